```python
import math
import jax, jax.numpy as jnp
from jax import lax
import numpy as np

D_MODEL = 4096
BATCH = 4
SEQ = 2048
DEPTH = 2
DEC_BATCH = 8
DEC_SEQ = 4
PAST_LEN = 16384
PAGE_SIZE = 128

N_HEADS = 32
HEAD_DIM = D_MODEL // N_HEADS
N_KV_HEADS = 8
GROUP = N_HEADS // N_KV_HEADS
ROT_DIM = HEAD_DIM // 4
ROPE_THETA = 500000.0
N_IDX_HEADS = 32
IDX_DIM = 128
IDX_ROT_DIM = IDX_DIM // 4
TOPK_MAX = 256
D_FF = ((8 * D_MODEL // 3 + 127) // 128) * 128
CONV_W = 3
N_META = 16
Q_BLOCK = 128
N_MIXERS = 2
N_A_LAYERS = (DEPTH + 1) // 2
N_B_LAYERS = DEPTH // 2
EPS = 1e-6
SCALE = HEAD_DIM ** -0.5
QW = N_HEADS * HEAD_DIM
KVW = N_KV_HEADS * HEAD_DIM
IQW = N_IDX_HEADS * IDX_DIM
A_COLS = QW + 2 * KVW + IQW + IDX_DIM + N_IDX_HEADS
B_COLS = QW + 2 * KVW
A_SPLITS = (QW, QW + KVW, QW + 2 * KVW, QW + 2 * KVW + IQW, QW + 2 * KVW + IQW + IDX_DIM)
B_SPLITS = (QW, QW + KVW)

kernel_name = 'meta_dsa_stickbreak_convglu_step'

F32 = jnp.float32


def rmsnorm(x, g):
    xf = x.astype(F32)
    y = xf * lax.rsqrt(jnp.mean(xf * xf, axis=-1, keepdims=True) + EPS)
    return (y * g.astype(F32)).astype(x.dtype)


def rope(x, pos, rot_dim):
    half = rot_dim // 2
    inv = ROPE_THETA ** (-jnp.arange(half, dtype=F32) / half)
    ang = pos.astype(F32)[:, None] * inv[None, :]
    cos = jnp.cos(ang)[:, None, :]
    sin = jnp.sin(ang)[:, None, :]
    xr = x[..., :rot_dim].astype(F32)
    x1, x2 = xr[..., :half], xr[..., half:]
    rot = jnp.concatenate([x1 * cos - x2 * sin, x2 * cos + x1 * sin], axis=-1)
    return jnp.concatenate([rot.astype(x.dtype), x[..., rot_dim:]], axis=-1)


def dsa_project(h, pos, w_in, q_gain, k_gain):
    B_, T = h.shape[:2]
    z = jnp.einsum('btd,de->bte', h, w_in)
    q, k, v, qi, ki, wi = jnp.split(z, A_SPLITS, axis=-1)
    q = rope(rmsnorm(q.reshape(B_, T, N_HEADS, HEAD_DIM), q_gain), pos, ROT_DIM)
    k = rope(rmsnorm(k.reshape(B_, T, N_KV_HEADS, HEAD_DIM), k_gain), pos, ROT_DIM)
    v = v.reshape(B_, T, N_KV_HEADS, HEAD_DIM)
    qi = rope(qi.reshape(B_, T, N_IDX_HEADS, IDX_DIM), pos, IDX_ROT_DIM)
    ki = rope(ki.reshape(B_, T, 1, IDX_DIM), pos, IDX_ROT_DIM)[:, :, 0]
    wi = wi * (N_IDX_HEADS ** -0.5) * (IDX_DIM ** -0.5)
    return q, k, v, qi, ki, wi


def indexer_scores(qi, wi, ki, qpos, kpos):
    logits = jnp.einsum('qhd,sd->qhs', qi, ki).astype(F32)
    score = jnp.einsum('qh,qhs->qs', wi.astype(F32), jax.nn.relu(logits))
    return jnp.where(kpos[None, :] <= qpos[:, None], score, -jnp.inf)


def gathered_attention(q, k_sel, v_sel, valid):
    Tq = q.shape[0]
    qg = q.reshape(Tq, N_KV_HEADS, GROUP, HEAD_DIM)
    s = jnp.einsum('qkgd,qnkd->qkgn', qg, k_sel).astype(F32) * SCALE
    s = jnp.where(valid[:, None, None, :], s, -jnp.inf)
    p = jax.nn.softmax(s, axis=-1)
    o = jnp.einsum('qkgn,qnkd->qkgd', p, v_sel.astype(F32))
    return o.reshape(Tq, QW)


def dsa_prompt_seq(args):
    q, k, v, qi, ki, wi = args
    T = q.shape[0]
    pos = jnp.arange(T)
    n_sel = min(TOPK_MAX, T // 4)

    def block(q_b, qi_b, wi_b, pos_b):
        score = indexer_scores(qi_b, wi_b, ki, pos_b, pos)
        _, idx = lax.top_k(score, n_sel)
        k_sel = jnp.take(k, idx, axis=0)
        v_sel = jnp.take(v, idx, axis=0)
        return gathered_attention(q_b, k_sel, v_sel, idx <= pos_b[:, None])

    meta_out = block(q[:N_META], qi[:N_META], wi[:N_META], pos[:N_META])
    nb = (T - N_META) // Q_BLOCK
    rb = lambda a: a[N_META:].reshape((nb, Q_BLOCK) + a.shape[1:])
    rest = lax.map(lambda a: block(a[0], a[1], a[2], a[3]), (rb(q), rb(qi), rb(wi), rb(pos)))
    return jnp.concatenate([meta_out, rest.reshape(nb * Q_BLOCK, QW)], axis=0)


def dsa_sample(q, k_new, v_new, qi, ki_new, wi, cache_k, cache_v, cache_ki, page_table):
    Bd, Ts = q.shape[:2]
    L = PAST_LEN + Ts
    n_sel = min(TOPK_MAX, L // 4)
    qpos = PAST_LEN + jnp.arange(Ts)
    kpos = jnp.arange(L)
    ki_past = cache_ki[page_table].reshape(Bd, PAST_LEN, IDX_DIM)
    ki_all = jnp.concatenate([ki_past, ki_new.astype(ki_past.dtype)], axis=1)
    score = jax.vmap(indexer_scores, in_axes=(0, 0, 0, None, None))(qi, wi, ki_all, qpos, kpos)
    _, idx = lax.top_k(score, n_sel)
    in_past = idx < PAST_LEN
    pidx = jnp.minimum(idx, PAST_LEN - 1)
    phys = jnp.take_along_axis(page_table, (pidx // PAGE_SIZE).reshape(Bd, -1), axis=1).reshape(idx.shape)
    off = pidx % PAGE_SIZE
    nidx = jnp.clip(idx - PAST_LEN, 0, Ts - 1)

    def select(cache, new):
        from_pages = cache[phys, off]
        from_new = jax.vmap(lambda a, i: a[i])(new, nidx)
        return jnp.where(in_past[..., None, None], from_pages, from_new.astype(from_pages.dtype))

    k_sel = select(cache_k, k_new)
    v_sel = select(cache_v, v_new)
    valid = idx <= qpos[None, :, None]
    return jax.vmap(gathered_attention)(q, k_sel, v_sel, valid)


def sb_project(h, w_in):
    B_, T = h.shape[:2]
    z = jnp.einsum('btd,de->bte', h, w_in)
    q, k, v = jnp.split(z, B_SPLITS, axis=-1)
    return (q.reshape(B_, T, N_HEADS, HEAD_DIM),
            k.reshape(B_, T, N_KV_HEADS, HEAD_DIM),
            v.reshape(B_, T, N_KV_HEADS, HEAD_DIM))


def sb_block(q_b, k, v, qpos, kpos):
    B_, Tq = q_b.shape[:2]
    qg = q_b.reshape(B_, Tq, N_KV_HEADS, GROUP, HEAD_DIM)
    z = jnp.einsum('bqkgd,bskd->bkgqs', qg, k).astype(F32) * SCALE
    mask = kpos[None, :] < qpos[:, None]
    log_rem = jnp.where(mask, jax.nn.log_sigmoid(-z), 0.0)
    suffix = lax.cumsum(log_rem, axis=4, reverse=True) - log_rem
    a = jnp.where(mask, jnp.exp(jax.nn.log_sigmoid(z) + suffix), 0.0)
    o = jnp.einsum('bkgqs,bskd->bqkgd', a, v.astype(F32))
    return o.reshape(B_, Tq, QW)


def sb_prompt(q, k, v):
    B_, T = q.shape[:2]
    pos = jnp.arange(T)
    meta_out = sb_block(q[:, :N_META], k, v, pos[:N_META], pos)
    nb = (T - N_META) // Q_BLOCK
    qb = q[:, N_META:].reshape(B_, nb, Q_BLOCK, N_HEADS, HEAD_DIM).swapaxes(0, 1)
    pb = pos[N_META:].reshape(nb, Q_BLOCK)
    rest = lax.map(lambda a: sb_block(a[0], k, v, a[1], pos), (qb, pb))
    rest = rest.swapaxes(0, 1).reshape(B_, nb * Q_BLOCK, QW)
    return jnp.concatenate([meta_out, rest], axis=1)


def sb_sample(q, k_new, v_new, cache_k, cache_v, page_table):
    Bd, Ts = q.shape[:2]
    qg = q.reshape(Bd, Ts, N_KV_HEADS, GROUP, HEAD_DIM)
    tpos = jnp.arange(Ts)
    mask = tpos[None, :] < tpos[:, None]
    z = jnp.einsum('bqkgd,bskd->bkgqs', qg, k_new).astype(F32) * SCALE
    log_rem = jnp.where(mask, jax.nn.log_sigmoid(-z), 0.0)
    suffix = lax.cumsum(log_rem, axis=4, reverse=True) - log_rem
    a = jnp.where(mask, jnp.exp(jax.nn.log_sigmoid(z) + suffix), 0.0)
    o = jnp.einsum('bkgqs,bskd->bqkgd', a, v_new.astype(F32))
    acc = jnp.sum(log_rem, axis=4)

    def page_step(carry, phys):
        o_c, acc_c = carry
        kp = cache_k[phys]
        vp = cache_v[phys]
        zp = jnp.einsum('bqkgd,bskd->bkgqs', qg, kp.astype(qg.dtype)).astype(F32) * SCALE
        lr = jax.nn.log_sigmoid(-zp)
        sfx = lax.cumsum(lr, axis=4, reverse=True) - lr + acc_c[..., None]
        ap = jnp.exp(jax.nn.log_sigmoid(zp) + sfx)
        o_c = o_c + jnp.einsum('bkgqs,bskd->bqkgd', ap, vp.astype(F32))
        return (o_c, acc_c + jnp.sum(lr, axis=4)), None

    (o, _), _ = lax.scan(page_step, (o, acc), page_table.T, reverse=True)
    return o.reshape(Bd, Ts, QW)


def conv_glu(h, conv_state, w_gate_up, conv_w, conv_b, w_down):
    T = h.shape[1]
    gu = jnp.einsum('btd,df->btf', h, w_gate_up)
    g, u = jnp.split(gu, 2, axis=-1)
    g_ext = jnp.concatenate([conv_state.astype(g.dtype), g], axis=1)
    gc = conv_b
    for i in range(CONV_W):
        gc = gc + g_ext[:, i:i + T] * conv_w[i]
    act = jax.nn.silu(gc.astype(F32)).astype(h.dtype) * u
    return jnp.einsum('btf,fd->btd', act, w_down), g_ext[:, -(CONV_W - 1):]


def setup_inputs(seed: int = 0) -> dict:
    key = jax.random.key(seed)
    ks = jax.random.split(key, 24)
    n_pages = PAST_LEN // PAGE_SIZE
    n_used = DEC_BATCH * n_pages
    n_pool = n_used + n_used // 4

    def nrm(k, shape, scale=1.0):
        return jax.random.normal(k, shape, F32) * scale

    page_table = jax.random.permutation(ks[0], n_pool)[:n_used].reshape(DEC_BATCH, n_pages).astype(jnp.int32)
    return {
        'x_prompt': nrm(ks[1], (BATCH, SEQ, D_MODEL)),
        'x_sample': nrm(ks[2], (DEC_BATCH, DEC_SEQ, D_MODEL)),
        'cache_k_a': nrm(ks[3], (N_A_LAYERS, n_pool, PAGE_SIZE, N_KV_HEADS, HEAD_DIM)),
        'cache_v_a': nrm(ks[4], (N_A_LAYERS, n_pool, PAGE_SIZE, N_KV_HEADS, HEAD_DIM)),
        'cache_kidx_a': nrm(ks[5], (N_A_LAYERS, n_pool, PAGE_SIZE, IDX_DIM)),
        'cache_k_b': nrm(ks[6], (N_B_LAYERS, n_pool, PAGE_SIZE, N_KV_HEADS, HEAD_DIM)),
        'cache_v_b': nrm(ks[7], (N_B_LAYERS, n_pool, PAGE_SIZE, N_KV_HEADS, HEAD_DIM)),
        'state_ffn_conv': nrm(ks[8], (DEPTH, DEC_BATCH, CONV_W - 1, D_FF)),
        'page_table': page_table,
        'meta_tokens': nrm(ks[9], (N_META, D_MODEL)),
        'norm_mix': 1.0 + nrm(ks[10], (DEPTH, D_MODEL), 0.01),
        'norm_ffn': 1.0 + nrm(ks[11], (DEPTH, D_MODEL), 0.01),
        'w_in_a': nrm(ks[12], (N_A_LAYERS, D_MODEL, A_COLS), D_MODEL ** -0.5),
        'q_norm_a': 1.0 + nrm(ks[13], (N_A_LAYERS, HEAD_DIM), 0.01),
        'k_norm_a': 1.0 + nrm(ks[14], (N_A_LAYERS, HEAD_DIM), 0.01),
        'w_out_a': nrm(ks[15], (N_A_LAYERS, QW, D_MODEL), QW ** -0.5),
        'w_in_b': nrm(ks[16], (N_B_LAYERS, D_MODEL, B_COLS), D_MODEL ** -0.5),
        'w_out_b': nrm(ks[17], (N_B_LAYERS, QW, D_MODEL), QW ** -0.5),
        'w_gate_up': nrm(ks[18], (DEPTH, D_MODEL, 2 * D_FF), D_MODEL ** -0.5),
        'conv_w': nrm(ks[19], (DEPTH, CONV_W, D_FF), CONV_W ** -0.5),
        'conv_b': nrm(ks[20], (DEPTH, D_FF), 0.01),
        'w_down': nrm(ks[21], (DEPTH, D_FF, D_MODEL), D_FF ** -0.5),
    }


def reference(x_prompt, x_sample, cache_k_a, cache_v_a, cache_kidx_a, cache_k_b, cache_v_b,
              state_ffn_conv, page_table, meta_tokens, norm_mix, norm_ffn, w_in_a, q_norm_a,
              k_norm_a, w_out_a, w_in_b, w_out_b, w_gate_up, conv_w, conv_b, w_down):
    B_ = x_prompt.shape[0]
    T = x_prompt.shape[1] + N_META
    Ts = x_sample.shape[1]
    meta = jnp.broadcast_to(meta_tokens[None].astype(x_prompt.dtype), (B_, N_META, D_MODEL))
    h_p = jnp.concatenate([meta, x_prompt], axis=1)
    h_s = x_sample
    pos_p = jnp.arange(T)
    pos_s = PAST_LEN + jnp.arange(Ts)

    ka_p, va_p, kia_p, kb_p, vb_p, conv_p = [], [], [], [], [], []
    ka_s, va_s, kia_s, kb_s, vb_s, conv_s = [], [], [], [], [], []
    for layer in range(DEPTH):
        j = layer // N_MIXERS
        n_p = rmsnorm(h_p, norm_mix[layer])
        n_s = rmsnorm(h_s, norm_mix[layer])
        if layer % N_MIXERS == 0:
            q, k, v, qi, ki, wi = dsa_project(n_p, pos_p, w_in_a[j], q_norm_a[j], k_norm_a[j])
            o_p = lax.map(dsa_prompt_seq, (q, k, v, qi, ki, wi))
            ka_p.append(k); va_p.append(v); kia_p.append(ki)
            q, k, v, qi, ki, wi = dsa_project(n_s, pos_s, w_in_a[j], q_norm_a[j], k_norm_a[j])
            o_s = dsa_sample(q, k, v, qi, ki, wi, cache_k_a[j], cache_v_a[j], cache_kidx_a[j], page_table)
            ka_s.append(k); va_s.append(v); kia_s.append(ki)
            w_out = w_out_a[j]
        else:
            q, k, v = sb_project(n_p, w_in_b[j])
            o_p = sb_prompt(q, k, v)
            kb_p.append(k); vb_p.append(v)
            q, k, v = sb_project(n_s, w_in_b[j])
            o_s = sb_sample(q, k, v, cache_k_b[j], cache_v_b[j], page_table)
            kb_s.append(k); vb_s.append(v)
            w_out = w_out_b[j]
        h_p = h_p + jnp.einsum('bte,ed->btd', o_p.astype(h_p.dtype), w_out)
        h_s = h_s + jnp.einsum('bte,ed->btd', o_s.astype(h_s.dtype), w_out)

        f_p, c_p = conv_glu(rmsnorm(h_p, norm_ffn[layer]), jnp.zeros((B_, CONV_W - 1, D_FF), h_p.dtype),
                            w_gate_up[layer], conv_w[layer], conv_b[layer], w_down[layer])
        f_s, c_s = conv_glu(rmsnorm(h_s, norm_ffn[layer]), state_ffn_conv[layer],
                            w_gate_up[layer], conv_w[layer], conv_b[layer], w_down[layer])
        h_p = h_p + f_p
        h_s = h_s + f_s
        conv_p.append(c_p); conv_s.append(c_s)

    return (h_p[:, N_META:], h_s,
            jnp.stack(ka_p), jnp.stack(va_p), jnp.stack(kia_p), jnp.stack(kb_p), jnp.stack(vb_p), jnp.stack(conv_p),
            jnp.stack(ka_s), jnp.stack(va_s), jnp.stack(kia_s), jnp.stack(kb_s), jnp.stack(vb_s), jnp.stack(conv_s))
```

```python
import functools
from typing import NamedTuple

import jax
import jax.numpy as jnp
from jax import lax
from jax.experimental import pallas as pl
from jax.experimental.pallas import tpu as pltpu

F32 = jnp.float32
BF16 = jnp.bfloat16
I32 = jnp.int32

TOPK_MAX = 256
ROPE_THETA = 500000.0
EPS = 1e-6
LANE = 128
SUBLANE = 8
TAIL = 128
NEG = -1e30
INT_MIN = -2 ** 31
VMEM_LIMIT = 56 * 1024 * 1024


class Dims(NamedTuple):
    B: int
    SEQ: int
    D: int
    Bd: int
    Ts: int
    n_meta: int
    page: int
    n_pages: int
    n_kv: int
    hd: int
    n_heads: int
    nih: int
    idim: int
    d_ff: int
    depth: int

    @property
    def NP(self):
        return self.B * self.SEQ

    @property
    def M(self):
        return self.NP + TAIL

    @property
    def QW(self):
        return self.n_heads * self.hd

    @property
    def KVW(self):
        return self.n_kv * self.hd

    @property
    def IQW(self):
        return self.nih * self.idim

    @property
    def G(self):
        return self.n_heads // self.n_kv

    @property
    def soff(self):
        return self.B * self.n_meta

    @property
    def past(self):
        return self.n_pages * self.page

    @property
    def nqb(self):
        return self.SEQ // TAIL

    @property
    def scale(self):
        return self.hd ** -0.5


def _cparams(n_axes):
    return pltpu.CompilerParams(dimension_semantics=("arbitrary",) * n_axes,
                                vmem_limit_bytes=VMEM_LIMIT)


def _pick_tile(total, target, align):
    best = None
    t = align
    while t <= min(total, target):
        if total % t == 0:
            best = t
        t += align
    assert best is not None, (total, target, align)
    return best


def _div(x, n):
    if n & (n - 1) == 0:
        return x >> (n.bit_length() - 1)
    return x // n


def _mod(x, n):
    if n & (n - 1) == 0:
        return x & (n - 1)
    return x % n


def _dot_nt(a, b):
    return lax.dot_general(a, b, (((1,), (1,)), ((), ())), preferred_element_type=F32)


def _dot(a, b):
    return jnp.dot(a, b, preferred_element_type=F32)


def _rmsnorm_kernel(x_ref, g_ref, o_ref):
    x = x_ref[...]
    ms = jnp.mean(x * x, axis=-1, keepdims=True)
    o_ref[...] = (x * lax.rsqrt(ms + EPS) * g_ref[...]).astype(o_ref.dtype)


def rmsnorm_bf16(h, gains, layer):
    M, D = h.shape
    tr = _pick_tile(M, 256, 16)
    return pl.pallas_call(
        _rmsnorm_kernel,
        grid=(M // tr,),
        in_specs=[pl.BlockSpec((tr, D), lambda m: (m, 0)),
                  pl.BlockSpec((None, 1, D), lambda m: (layer, 0, 0))],
        out_specs=pl.BlockSpec((tr, D), lambda m: (m, 0)),
        out_shape=jax.ShapeDtypeStruct((M, D), BF16),
        compiler_params=_cparams(1),
    )(h, gains[:, None, :])


def _matmul_kernel(*refs, has_resid, emit_bf16):
    x_ref, w_ref = refs[0], refs[1]
    pos = 2
    r_ref = None
    if has_resid:
        r_ref = refs[pos]
        pos += 1
    o_ref = refs[pos]
    pos += 1
    ob_ref = None
    if emit_bf16:
        ob_ref = refs[pos]
        pos += 1
    wb_ref = refs[pos]

    @pl.when(pl.program_id(1) == 0)
    def _():
        wb_ref[...] = w_ref[...].astype(BF16)

    acc = _dot(x_ref[...], wb_ref[...])
    if has_resid:
        acc = acc + r_ref[...]
    o_ref[...] = acc
    if emit_bf16:
        ob_ref[...] = acc.astype(BF16)


def matmul(x, w, layer, *, k_block=0, k_blocks=1, resid=None, emit_bf16=False):
    M, Kx = x.shape
    _, Kw, N = w.shape
    assert Kx == Kw and Kx % k_blocks == 0
    K = Kx // k_blocks
    assert K % LANE == 0
    tm = _pick_tile(M, 640, 16)
    tn_target = 512 if K <= 4096 else 256
    tn = tn_target
    for cand in (tn_target, 256, 128):
        if cand <= tn_target and N % cand == 0:
            tn = cand
            break
    grid = (pl.cdiv(N, tn), M // tm)
    in_specs = [pl.BlockSpec((tm, K), lambda n, m: (m, k_block)),
                pl.BlockSpec((None, K, tn), lambda n, m: (layer, k_block, n))]
    args = [x, w]
    if resid is not None:
        in_specs.append(pl.BlockSpec((tm, tn), lambda n, m: (m, n)))
        args.append(resid)
    out_specs = [pl.BlockSpec((tm, tn), lambda n, m: (m, n))]
    out_shape = [jax.ShapeDtypeStruct((M, N), F32)]
    if emit_bf16:
        out_specs.append(pl.BlockSpec((tm, tn), lambda n, m: (m, n)))
        out_shape.append(jax.ShapeDtypeStruct((M, N), BF16))
    res = pl.pallas_call(
        functools.partial(_matmul_kernel, has_resid=resid is not None, emit_bf16=emit_bf16),
        grid=grid,
        in_specs=in_specs,
        out_specs=out_specs,
        out_shape=out_shape,
        scratch_shapes=[pltpu.VMEM((K, tn), BF16)],
        compiler_params=_cparams(2),
    )(*args)
    return res if emit_bf16 else res[0]


def _rope(x, c, s1, s2, rot):
    half = rot // 2
    return x * c + pltpu.roll(x, half, 1) * s1 + pltpu.roll(x, LANE - half, 1) * s2


def _dsa_post_kernel(z_ref, qg_ref, kg_ref, c_ref, s1_ref, s2_ref,
                     q_ref, kf_ref, kb_ref, vb_ref, qi_ref, kif_ref, kib_ref, wi_ref, *, d):
    c, s1, s2 = c_ref[...], s1_ref[...], s2_ref[...]
    rot = d.hd // 4
    hd = d.hd

    def headnorm(x, g):
        ms = jnp.mean(x * x, axis=-1, keepdims=True)
        return x * lax.rsqrt(ms + EPS) * g

    qg, kg = qg_ref[...], kg_ref[...]
    for h in range(d.n_heads):
        x = z_ref[:, h * hd:(h + 1) * hd]
        q_ref[:, h * hd:(h + 1) * hd] = _rope(headnorm(x, qg), c, s1, s2, rot).astype(BF16)
    off = d.QW
    for h in range(d.n_kv):
        x = z_ref[:, off + h * hd: off + (h + 1) * hd]
        y = _rope(headnorm(x, kg), c, s1, s2, rot)
        kf_ref[:, h * hd:(h + 1) * hd] = y
        kb_ref[:, h * hd:(h + 1) * hd] = y.astype(BF16)
    off = d.QW + d.KVW
    vb_ref[...] = z_ref[:, off: off + d.KVW].astype(BF16)
    off = d.QW + 2 * d.KVW
    for h in range(d.nih):
        x = z_ref[:, off + h * d.idim: off + (h + 1) * d.idim]
        qi_ref[h] = _rope(x, c, s1, s2, rot).astype(BF16)
    off = d.QW + 2 * d.KVW + d.IQW
    y = _rope(z_ref[:, off: off + d.idim], c, s1, s2, rot)
    kif_ref[...] = y
    kib_ref[...] = y.astype(BF16)
    off = off + d.idim
    wi_ref[...] = z_ref[:, off: off + d.nih] * (d.nih ** -0.5) * (d.idim ** -0.5)


def dsa_post(z, q_gain, k_gain, layer, tabs, d):
    M = d.M
    tr = TAIL
    c, s1, s2 = tabs
    row = lambda w: pl.BlockSpec((tr, w), lambda m: (m, 0))
    gain = pl.BlockSpec((None, 1, d.hd), lambda m: (layer, 0, 0))
    return pl.pallas_call(
        functools.partial(_dsa_post_kernel, d=d),
        grid=(M // tr,),
        in_specs=[row(z.shape[1]), gain, gain, row(LANE), row(LANE), row(LANE)],
        out_specs=[row(d.QW), row(d.KVW), row(d.KVW), row(d.KVW),
                   pl.BlockSpec((d.nih, tr, d.idim), lambda m: (0, m, 0)),
                   row(d.idim), row(d.idim), row(d.nih)],
        out_shape=[jax.ShapeDtypeStruct((M, d.QW), BF16),
                   jax.ShapeDtypeStruct((M, d.KVW), F32),
                   jax.ShapeDtypeStruct((M, d.KVW), BF16),
                   jax.ShapeDtypeStruct((M, d.KVW), BF16),
                   jax.ShapeDtypeStruct((d.nih, M, d.idim), BF16),
                   jax.ShapeDtypeStruct((M, d.idim), F32),
                   jax.ShapeDtypeStruct((M, d.idim), BF16),
                   jax.ShapeDtypeStruct((M, d.nih), F32)],
        compiler_params=_cparams(1),
    )(z, q_gain[:, None, :], k_gain[:, None, :], c, s1, s2)


def _float_key(x):
    bits = lax.bitcast_convert_type(x, I32)
    return bits ^ ((bits >> 31) & jnp.int32(0x7FFFFFFF))


def _kth_largest_key(key, k):
    kf = jnp.float32(k)

    def count_ge(cand):
        return jnp.sum(jnp.where(key >= cand, 1.0, 0.0), axis=1, keepdims=True)

    t0 = jnp.where(count_ge(jnp.int32(0)) >= kf, jnp.int32(0), jnp.int32(INT_MIN))

    def body(it, t):
        cand = t | lax.shift_left(jnp.int32(1), jnp.int32(30) - it)
        return jnp.where(count_ge(cand) >= kf, cand, t)

    return lax.fori_loop(0, 31, body, t0)


def _stack_heads(x, n, w):
    return jnp.concatenate([x[:, j * w:(j + 1) * w] for j in range(n)], axis=0)


def _prompt_query_meta(s, d):
    is_tail = s >= d.B * d.nqb
    b = jnp.minimum(s // d.nqb, d.B - 1)
    i = s % d.nqb
    r = lax.broadcasted_iota(I32, (TAIL, 1), 0)
    q_seq = jnp.where(is_tail, jnp.where(r < d.soff, _div(r, d.n_meta), -1), b)
    q_pos = jnp.where(is_tail, _mod(r, d.n_meta), d.n_meta + i * TAIL + r)
    return is_tail, b, i, q_seq, q_pos


def _tail_key_meta(d):
    c = lax.broadcasted_iota(I32, (1, TAIL), 1)
    k_seq = jnp.where(c < d.soff, _div(c, d.n_meta), -2)
    k_pos = _mod(c, d.n_meta)
    return k_seq, k_pos


def _dsa_prompt_kernel(qi_ref, wi_ref, kip_ref, kit_ref, q_ref, kp_ref, kt_ref, vp_ref, vt_ref,
                       o_ref, score_ref, sel_ref, *, d, n_sel):
    s = pl.program_id(0)
    g = pl.program_id(1)
    SEQ = d.SEQ

    @pl.when(g == 0)
    def _():
        _, b, _, q_seq, q_pos = _prompt_query_meta(s, d)
        cp = lax.broadcasted_iota(I32, (1, SEQ), 1)
        valid_p = (q_seq == b) & (d.n_meta + cp <= q_pos)
        tk_seq, tk_pos = _tail_key_meta(d)
        valid_t = (tk_seq == q_seq) & (tk_pos <= q_pos)
        valid = jnp.concatenate([valid_p, valid_t], axis=1)

        score_ref[...] = jnp.zeros_like(score_ref)
        wi = wi_ref[...]
        lane = lax.broadcasted_iota(I32, wi.shape, 1)

        def head(h, carry):
            qh = qi_ref[h]
            logit = jnp.concatenate([_dot_nt(qh, kip_ref[...]), _dot_nt(qh, kit_ref[...])], axis=1)
            wcol = jnp.sum(jnp.where(lane == h, wi, 0.0), axis=1, keepdims=True)
            score_ref[...] += wcol * jnp.maximum(logit, 0.0)
            return carry

        lax.fori_loop(0, d.nih, head, 0)
        key = jnp.where(valid, _float_key(score_ref[...]), jnp.int32(INT_MIN))
        t = _kth_largest_key(key, n_sel)
        sel_ref[...] = jnp.where(valid & (key >= t), 1.0, 0.0)

    G = d.G
    qg = _stack_heads(q_ref[...], G, d.hd)
    sc = jnp.concatenate([_dot_nt(qg, kp_ref[...]), _dot_nt(qg, kt_ref[...])], axis=1) * d.scale
    sel = jnp.concatenate([sel_ref[...]] * G, axis=0) > 0.5
    sc = jnp.where(sel, sc, NEG)
    m = jnp.max(sc, axis=1, keepdims=True)
    p = jnp.exp(sc - m)
    l = jnp.sum(p, axis=1, keepdims=True)
    pb = p.astype(BF16)
    o = (_dot(pb[:, :SEQ], vp_ref[...]) + _dot(pb[:, SEQ:], vt_ref[...])) / l
    for j in range(G):
        o_ref[:, j * d.hd:(j + 1) * d.hd] = o[j * TAIL:(j + 1) * TAIL].astype(o_ref.dtype)


def dsa_prompt_attention(q, kb, vb, qi, kib, wi, d):
    n_sel = min(TOPK_MAX, (d.SEQ + d.n_meta) // 4)
    nblk = d.B * d.nqb + 1
    tailb = d.NP // TAIL
    gw = d.G * d.hd
    seq_of = lambda s: jnp.minimum(s // d.nqb, d.B - 1)
    in_specs = [
        pl.BlockSpec((d.nih, TAIL, d.idim), lambda s, g: (0, s, 0)),
        pl.BlockSpec((TAIL, d.nih), lambda s, g: (s, 0)),
        pl.BlockSpec((d.SEQ, d.idim), lambda s, g: (seq_of(s), 0)),
        pl.BlockSpec((TAIL, d.idim), lambda s, g: (tailb, 0)),
        pl.BlockSpec((TAIL, gw), lambda s, g: (s, g)),
        pl.BlockSpec((d.SEQ, d.hd), lambda s, g: (seq_of(s), g)),
        pl.BlockSpec((TAIL, d.hd), lambda s, g: (tailb, g)),
        pl.BlockSpec((d.SEQ, d.hd), lambda s, g: (seq_of(s), g)),
        pl.BlockSpec((TAIL, d.hd), lambda s, g: (tailb, g)),
    ]
    return pl.pallas_call(
        functools.partial(_dsa_prompt_kernel, d=d, n_sel=n_sel),
        grid=(nblk, d.n_kv),
        in_specs=in_specs,
        out_specs=pl.BlockSpec((TAIL, gw), lambda s, g: (s, g)),
        out_shape=jax.ShapeDtypeStruct((d.M, d.QW), BF16),
        scratch_shapes=[pltpu.VMEM((TAIL, d.SEQ + TAIL), F32),
                        pltpu.VMEM((TAIL, d.SEQ + TAIL), F32)],
        compiler_params=_cparams(2),
    )(qi, wi, kib, kib, q, kb, kb, vb, vb)


def _log_sigmoid(z):
    return jnp.minimum(z, 0.0) - jnp.log1p(jnp.exp(-jnp.abs(z)))


def _sb_block(z, valid, vblk, carry):
    ls = _log_sigmoid(z)
    lr = jnp.where(valid, ls - z, 0.0)
    ci = lax.broadcasted_iota(I32, (LANE, LANE), 0)
    cj = lax.broadcasted_iota(I32, (LANE, LANE), 1)
    upper = jnp.where(ci > cj, 1.0, 0.0).astype(BF16)
    hi = lr.astype(BF16)
    lo = (lr - hi.astype(F32)).astype(BF16)
    sfx = _dot(hi, upper) + _dot(lo, upper)
    a = jnp.where(valid, jnp.exp(ls + sfx + carry), 0.0)
    return _dot(a.astype(BF16), vblk), jnp.sum(lr, axis=1, keepdims=True)


def _sb_prompt_kernel(q_ref, kp_ref, kt_ref, vp_ref, vt_ref, o_ref, *, d):
    s = pl.program_id(0)
    G = d.G
    R = G * TAIL
    is_tail, _, i, q_seq, q_pos = _prompt_query_meta(s, d)
    q_seq = jnp.concatenate([q_seq] * G, axis=0)
    q_pos = jnp.concatenate([q_pos] * G, axis=0)
    qg = _stack_heads(q_ref[...], G, d.hd)
    cl = lax.broadcasted_iota(I32, (1, TAIL), 1)

    def prompt_block(j, carry):
        acc, run = carry
        kb = i - j
        start = pl.multiple_of(kb * TAIL, TAIL)
        kblk = kp_ref[pl.ds(start, TAIL), :]
        vblk = vp_ref[pl.ds(start, TAIL), :]
        z = _dot_nt(qg, kblk) * d.scale
        valid = (d.n_meta + kb * TAIL + cl) < q_pos
        o, tot = _sb_block(z, valid, vblk, run)
        return acc + o, run + tot

    n_blocks = jnp.where(is_tail, 0, i + 1)
    acc, run = lax.fori_loop(0, n_blocks, prompt_block,
                             (jnp.zeros((R, d.hd), F32), jnp.zeros((R, 1), F32)))
    tk_seq, tk_pos = _tail_key_meta(d)
    valid = (tk_seq == q_seq) & (tk_pos < q_pos)
    z = _dot_nt(qg, kt_ref[...]) * d.scale
    o, _ = _sb_block(z, valid, vt_ref[...], run)
    acc = acc + o
    for j in range(G):
        o_ref[:, j * d.hd:(j + 1) * d.hd] = acc[j * TAIL:(j + 1) * TAIL].astype(o_ref.dtype)


def sb_prompt_attention(qb, kb, vb, d):
    zb, k_off, v_off = qb, kb, vb
    nblk = d.B * d.nqb + 1
    tailb = d.NP // TAIL
    gw = d.G * d.hd
    seq_of = lambda s: jnp.minimum(s // d.nqb, d.B - 1)
    in_specs = [
        pl.BlockSpec((TAIL, gw), lambda s, g: (s, g)),
        pl.BlockSpec((d.SEQ, d.hd), lambda s, g: (seq_of(s), k_off + g)),
        pl.BlockSpec((TAIL, d.hd), lambda s, g: (tailb, k_off + g)),
        pl.BlockSpec((d.SEQ, d.hd), lambda s, g: (seq_of(s), v_off + g)),
        pl.BlockSpec((TAIL, d.hd), lambda s, g: (tailb, v_off + g)),
    ]
    return pl.pallas_call(
        functools.partial(_sb_prompt_kernel, d=d),
        grid=(nblk, d.n_kv),
        in_specs=in_specs,
        out_specs=pl.BlockSpec((TAIL, gw), lambda s, g: (s, g)),
        out_shape=jax.ShapeDtypeStruct((d.M, d.QW), BF16),
        compiler_params=_cparams(2),
    )(zb, zb, zb, zb, zb)


def _block_diag_q(q, d):
    R = q.shape[0]
    rho = lax.broadcasted_iota(I32, (R, 1), 0)
    grp = _div(_mod(rho, d.n_heads), d.G)
    zero = jnp.zeros_like(q)
    return jnp.concatenate([jnp.where(grp == g, q, zero) for g in range(d.n_kv)], axis=1)


def _own_group_slab(oall, d):
    R = oall.shape[0]
    rho = lax.broadcasted_iota(I32, (R, 1), 0)
    grp = _div(_mod(rho, d.n_heads), d.G)
    out = jnp.zeros((R, d.hd), F32)
    for g in range(d.n_kv):
        out = out + jnp.where(grp == g, oall[:, g * d.hd:(g + 1) * d.hd], 0.0)
    return out


def _sample_idx_kernel(pt_ref, qi_ref, wi_ref, kc_ref, kt_ref, o_ref, *, d):
    p = pl.program_id(1)

    def compute(kblk):
        w = jnp.maximum(_dot_nt(qi_ref[...], kblk), 0.0) * wi_ref[...]
        rows = [jnp.sum(w[t * d.nih:(t + 1) * d.nih], axis=0, keepdims=True) for t in range(d.Ts)]
        rows.append(jnp.zeros((SUBLANE - d.Ts, LANE), F32))
        o_ref[...] = jnp.concatenate(rows, axis=0)

    @pl.when(p < d.n_pages)
    def _():
        compute(kc_ref[...].astype(BF16))

    @pl.when(p == d.n_pages)
    def _():
        compute(kt_ref[...])


def sample_idx_scores(page_table, qi_s, wi_s, cache_ki, layer, kib, d):
    R = d.Ts * d.nih
    npg = d.n_pages
    tailb = d.NP // TAIL
    grid_spec = pltpu.PrefetchScalarGridSpec(
        num_scalar_prefetch=1,
        grid=(d.Bd, npg + 1),
        in_specs=[
            pl.BlockSpec((None, R, d.idim), lambda b, p, pt: (b, 0, 0)),
            pl.BlockSpec((None, R, 1), lambda b, p, pt: (b, 0, 0)),
            pl.BlockSpec((None, None, d.page, d.idim),
                         lambda b, p, pt: (layer, pt[b, jnp.minimum(p, npg - 1)], 0, 0)),
            pl.BlockSpec((TAIL, d.idim), lambda b, p, pt: (tailb, 0)),
        ],
        out_specs=pl.BlockSpec((None, SUBLANE, LANE), lambda b, p, pt: (b, 0, p)),
    )
    return pl.pallas_call(
        functools.partial(_sample_idx_kernel, d=d),
        grid_spec=grid_spec,
        out_shape=jax.ShapeDtypeStruct((d.Bd, SUBLANE, (npg + 1) * LANE), F32),
        compiler_params=_cparams(2),
    )(page_table, qi_s, wi_s, cache_ki, kib)


def _sample_select_kernel(s_ref, o_ref, *, d, n_sel):
    b = pl.program_id(0)
    L = s_ref.shape[1]
    col = lax.broadcasted_iota(I32, (1, L), 1)
    t = lax.broadcasted_iota(I32, (SUBLANE, 1), 0)
    snew = col - d.past - d.soff - d.Ts * b
    valid = (col < d.past) | ((snew >= 0) & (snew < d.Ts) & (snew <= t))
    valid = valid & (t < d.Ts)
    key = jnp.where(valid, _float_key(s_ref[...]), jnp.int32(INT_MIN))
    thr = _kth_largest_key(key, n_sel)
    o_ref[...] = jnp.where(valid & (key >= thr), 1.0, 0.0)


def sample_select(scores, d):
    n_sel = min(TOPK_MAX, (d.past + d.Ts) // 4)
    L = scores.shape[2]
    spec = pl.BlockSpec((None, SUBLANE, L), lambda b: (b, 0, 0))
    return pl.pallas_call(
        functools.partial(_sample_select_kernel, d=d, n_sel=n_sel),
        grid=(d.Bd,),
        in_specs=[spec],
        out_specs=spec,
        out_shape=jax.ShapeDtypeStruct(scores.shape, F32),
        compiler_params=_cparams(1),
    )(scores)


def _sample_dsa_attn_kernel(pt_ref, q_ref, sel_ref, kc_ref, vc_ref, kt_ref, vt_ref, o_ref,
                            qbd_ref, m_ref, l_ref, acc_ref, *, d):
    p = pl.program_id(1)

    @pl.when(p == 0)
    def _():
        qbd_ref[...] = _block_diag_q(q_ref[...], d)
        m_ref[...] = jnp.full_like(m_ref, NEG)
        l_ref[...] = jnp.zeros_like(l_ref)
        acc_ref[...] = jnp.zeros_like(acc_ref)

    def step(kblk, vblk):
        sc = _dot_nt(qbd_ref[...], kblk) * d.scale
        sel = sel_ref[...]
        selr = jnp.concatenate(
            [jnp.broadcast_to(sel[t:t + 1], (d.n_heads, LANE)) for t in range(d.Ts)], axis=0) > 0.5
        sc = jnp.where(selr, sc, NEG)
        m_old = m_ref[...]
        m_new = jnp.maximum(m_old, jnp.max(sc, axis=1, keepdims=True))
        alpha = jnp.exp(m_old - m_new)
        pr = jnp.where(selr, jnp.exp(sc - m_new), 0.0)
        l_ref[...] = alpha * l_ref[...] + jnp.sum(pr, axis=1, keepdims=True)
        od = _own_group_slab(_dot(pr.astype(BF16), vblk), d)
        acc_ref[...] = alpha * acc_ref[...] + od
        m_ref[...] = m_new

    @pl.when(p < d.n_pages)
    def _():
        step(kc_ref[...].astype(BF16), vc_ref[...].astype(BF16))

    @pl.when(p == d.n_pages)
    def _():
        step(kt_ref[...], vt_ref[...])
        o_ref[...] = (acc_ref[...] / l_ref[...]).astype(o_ref.dtype)


def sample_dsa_attention(page_table, q_s, sel, cache_k, cache_v, layer, kb, vb, d):
    R = d.Ts * d.n_heads
    npg = d.n_pages
    tailb = d.NP // TAIL
    page_spec = pl.BlockSpec((None, None, d.page, d.KVW),
                             lambda b, p, pt: (layer, pt[b, jnp.minimum(p, npg - 1)], 0, 0))
    tail_spec = pl.BlockSpec((TAIL, d.KVW), lambda b, p, pt: (tailb, 0))
    grid_spec = pltpu.PrefetchScalarGridSpec(
        num_scalar_prefetch=1,
        grid=(d.Bd, npg + 1),
        in_specs=[
            pl.BlockSpec((None, R, d.hd), lambda b, p, pt: (b, 0, 0)),
            pl.BlockSpec((None, SUBLANE, LANE), lambda b, p, pt: (b, 0, p)),
            page_spec, page_spec, tail_spec, tail_spec,
        ],
        out_specs=pl.BlockSpec((None, R, d.hd), lambda b, p, pt: (b, 0, 0)),
        scratch_shapes=[pltpu.VMEM((R, d.KVW), BF16), pltpu.VMEM((R, 1), F32),
                        pltpu.VMEM((R, 1), F32), pltpu.VMEM((R, d.hd), F32)],
    )
    return pl.pallas_call(
        functools.partial(_sample_dsa_attn_kernel, d=d),
        grid_spec=grid_spec,
        out_shape=jax.ShapeDtypeStruct((d.Bd, R, d.hd), BF16),
        compiler_params=_cparams(2),
    )(page_table, q_s, sel, cache_k, cache_v, kb, vb)


def _sample_sb_kernel(pt_ref, q_ref, kc_ref, vc_ref, kt_ref, vt_ref, o_ref,
                      qbd_ref, run_ref, acc_ref, *, d):
    b = pl.program_id(0)
    p = pl.program_id(1)
    R = q_ref.shape[0]

    def step(kblk, vblk, valid, run):
        z = _dot_nt(qbd_ref[...], kblk) * d.scale
        ls = _log_sigmoid(z)
        lr = ls - z
        if valid is not None:
            lr = jnp.where(valid, lr, 0.0)
        ci = lax.broadcasted_iota(I32, (LANE, LANE), 0)
        cj = lax.broadcasted_iota(I32, (LANE, LANE), 1)
        upper = jnp.where(ci > cj, 1.0, 0.0).astype(BF16)
        hi = lr.astype(BF16)
        lo = (lr - hi.astype(F32)).astype(BF16)
        sfx = _dot(hi, upper) + _dot(lo, upper)
        a = jnp.exp(ls + sfx + run)
        if valid is not None:
            a = jnp.where(valid, a, 0.0)
        od = _own_group_slab(_dot(a.astype(BF16), vblk), d)
        return od, jnp.sum(lr, axis=1, keepdims=True)

    @pl.when(p == 0)
    def _():
        qbd_ref[...] = _block_diag_q(q_ref[...], d)
        rho = lax.broadcasted_iota(I32, (R, 1), 0)
        tq = _div(rho, d.n_heads)
        c = lax.broadcasted_iota(I32, (1, TAIL), 1)
        snew = c - d.soff - d.Ts * b
        valid = (snew >= 0) & (snew < tq)
        od, tot = step(kt_ref[...], vt_ref[...], valid, jnp.zeros((R, 1), F32))
        acc_ref[...] = od
        run_ref[...] = tot

    @pl.when(p > 0)
    def _():
        od, tot = step(kc_ref[...].astype(BF16), vc_ref[...].astype(BF16), None, run_ref[...])
        acc_ref[...] += od
        run_ref[...] += tot

    @pl.when(p == d.n_pages)
    def _():
        o_ref[...] = acc_ref[...].astype(o_ref.dtype)


def sample_sb_attention(page_table, q_s, cache_k, cache_v, layer, zb, k_off, v_off, d):
    R = d.Ts * d.n_heads
    npg = d.n_pages
    tailb = d.NP // TAIL
    page_spec = pl.BlockSpec(
        (None, None, d.page, d.KVW),
        lambda b, p, pt: (layer, pt[b, jnp.clip(npg - p, 0, npg - 1)], 0, 0))
    grid_spec = pltpu.PrefetchScalarGridSpec(
        num_scalar_prefetch=1,
        grid=(d.Bd, npg + 1),
        in_specs=[
            pl.BlockSpec((None, R, d.hd), lambda b, p, pt: (b, 0, 0)),
            page_spec, page_spec,
            pl.BlockSpec((TAIL, d.KVW), lambda b, p, pt: (tailb, k_off)),
            pl.BlockSpec((TAIL, d.KVW), lambda b, p, pt: (tailb, v_off)),
        ],
        out_specs=pl.BlockSpec((None, R, d.hd), lambda b, p, pt: (b, 0, 0)),
        scratch_shapes=[pltpu.VMEM((R, d.KVW), BF16), pltpu.VMEM((R, 1), F32),
                        pltpu.VMEM((R, d.hd), F32)],
    )
    return pl.pallas_call(
        functools.partial(_sample_sb_kernel, d=d),
        grid_spec=grid_spec,
        out_shape=jax.ShapeDtypeStruct((d.Bd, R, d.hd), BF16),
        compiler_params=_cparams(2),
    )(page_table, q_s, cache_k, cache_v, zb, zb)


def _conv_glu_kernel(g_ref, halo_ref, u_ref, w_ref, b_ref, p1_ref, p2_ref, o_ref, *, d):
    m = pl.program_id(1)
    is_tail = m == d.NP // TAIL
    g = g_ref[...]
    gext = jnp.concatenate([halo_ref[...], g], axis=0)
    g1 = pltpu.roll(gext, 1, 0)[SUBLANE:]
    g2 = pltpu.roll(gext, 2, 0)[SUBLANE:]
    r = lax.broadcasted_iota(I32, (TAIL, 1), 0)
    step = jnp.where(r < d.soff, _mod(r, d.n_meta), _mod(r - d.soff, d.Ts))
    g1 = jnp.where(is_tail & (step < 1), p1_ref[...], g1)
    g2 = jnp.where(is_tail & (step < 2), p2_ref[...], g2)
    gc = b_ref[...] + g2 * w_ref[0:1, :] + g1 * w_ref[1:2, :] + g * w_ref[2:3, :]
    o_ref[...] = (gc * jax.nn.sigmoid(gc) * u_ref[...]).astype(o_ref.dtype)


def conv_glu_gate(gu, conv_w, conv_b, layer, prev1, prev2, d):
    assert conv_w.shape[1] == 3
    M = d.M
    tf = _pick_tile(d.d_ff, 5504, LANE)
    nf = d.d_ff // tf
    per_seq = d.nqb
    tail_m = d.NP // TAIL

    def halo_idx(f, m):
        b = m // per_seq
        seq_start = (d.NP + b * d.n_meta + d.n_meta - SUBLANE) // SUBLANE
        normal = (TAIL // SUBLANE) * m - 1
        idx = jnp.where(m % per_seq == 0, seq_start, normal)
        return (jnp.where(m == tail_m, 0, idx), f)

    return pl.pallas_call(
        functools.partial(_conv_glu_kernel, d=d),
        grid=(nf, M // TAIL),
        in_specs=[
            pl.BlockSpec((TAIL, tf), lambda f, m: (m, f)),
            pl.BlockSpec((SUBLANE, tf), halo_idx),
            pl.BlockSpec((TAIL, tf), lambda f, m: (m, nf + f)),
            pl.BlockSpec((None, 3, tf), lambda f, m: (layer, 0, f)),
            pl.BlockSpec((None, 1, tf), lambda f, m: (layer, 0, f)),
            pl.BlockSpec((TAIL, tf), lambda f, m: (0, f)),
            pl.BlockSpec((TAIL, tf), lambda f, m: (0, f)),
        ],
        out_specs=pl.BlockSpec((TAIL, tf), lambda f, m: (m, f)),
        out_shape=jax.ShapeDtypeStruct((M, d.d_ff), BF16),
        compiler_params=_cparams(2),
    )(gu, gu, gu, conv_w, conv_b[:, None, :], prev1, prev2)


def _rope_tables(pos, hd):
    rot = hd // 4
    half = rot // 2
    inv = ROPE_THETA ** (-jnp.arange(half, dtype=F32) / half)
    ang = pos.astype(F32)[:, None] * inv[None, :]
    cos, sin = jnp.cos(ang), jnp.sin(ang)
    n = pos.shape[0]
    ones = jnp.ones((n, hd - rot), F32)
    zeros = jnp.zeros((n, hd - rot), F32)
    zh = jnp.zeros((n, half), F32)
    c = jnp.concatenate([cos, cos, ones], axis=1)
    s1 = jnp.concatenate([zh, sin, zeros], axis=1)
    s2 = jnp.concatenate([-sin, zh, zeros], axis=1)
    return c, s1, s2


def _to_seq(rows, d):
    W = rows.shape[1]
    meta = rows[d.NP: d.NP + d.soff].reshape(d.B, d.n_meta, W)
    prm = rows[: d.NP].reshape(d.B, d.SEQ, W)
    smp = rows[d.NP + d.soff: d.NP + d.soff + d.Bd * d.Ts].reshape(d.Bd, d.Ts, W)
    return jnp.concatenate([meta, prm], axis=1), smp


def _sample_rows(x, d, per_row):
    w = x.shape[1] // per_row
    lo = d.NP + d.soff
    return x[lo: lo + d.Bd * d.Ts].reshape(d.Bd, d.Ts * per_row, w)


def _with_sample_rows(o, o_s, d):
    lo = d.NP + d.soff
    return jnp.concatenate([o[:lo], o_s.reshape(d.Bd * d.Ts, d.QW), o[lo + d.Bd * d.Ts:]], axis=0)


def kernel(x_prompt, x_sample, cache_k_a, cache_v_a, cache_kidx_a, cache_k_b, cache_v_b,
           state_ffn_conv, page_table, meta_tokens, norm_mix, norm_ffn, w_in_a, q_norm_a,
           k_norm_a, w_out_a, w_in_b, w_out_b, w_gate_up, conv_w, conv_b, w_down):
    B, SEQ, D = x_prompt.shape
    Bd, Ts, _ = x_sample.shape
    n_meta = meta_tokens.shape[0]
    _, n_pool, page, n_kv, hd = cache_k_a.shape
    idim = cache_kidx_a.shape[-1]
    QW = w_out_a.shape[1]
    nih = (w_in_a.shape[2] - QW - 2 * n_kv * hd - idim) // idim
    d = Dims(B=B, SEQ=SEQ, D=D, Bd=Bd, Ts=Ts, n_meta=n_meta, page=page,
             n_pages=page_table.shape[1], n_kv=n_kv, hd=hd, n_heads=QW // hd, nih=nih,
             idim=idim, d_ff=w_down.shape[1], depth=norm_mix.shape[0])
    assert SEQ % TAIL == 0 and page == LANE and hd == LANE and idim == LANE
    assert d.soff + Bd * Ts <= TAIL and n_meta % SUBLANE == 0 and 2 <= Ts <= SUBLANE
    assert w_in_a.shape[2] == d.QW + 2 * d.KVW + d.IQW + idim + nih
    assert (Ts * d.nih) % SUBLANE == 0 and d.nih % SUBLANE == 0
    npad = TAIL - d.soff - Bd * Ts

    h = jnp.concatenate([x_prompt.reshape(d.NP, D), jnp.tile(meta_tokens, (B, 1)),
                         x_sample.reshape(Bd * Ts, D), jnp.zeros((npad, D), F32)], axis=0)
    pos = jnp.concatenate([jnp.tile(n_meta + jnp.arange(SEQ), B), jnp.tile(jnp.arange(n_meta), B),
                           jnp.tile(d.past + jnp.arange(Ts), Bd), jnp.zeros((npad,), I32)])
    tabs = _rope_tables(pos, hd)

    ck_a = cache_k_a.reshape(cache_k_a.shape[:3] + (d.KVW,))
    cv_a = cache_v_a.reshape(cache_v_a.shape[:3] + (d.KVW,))
    ck_b = cache_k_b.reshape(cache_k_b.shape[:3] + (d.KVW,))
    cv_b = cache_v_b.reshape(cache_v_b.shape[:3] + (d.KVW,))

    outs = {k: [] for k in ("ka_p", "va_p", "ki_p", "kb_p", "vb_p", "cv_p",
                            "ka_s", "va_s", "ki_s", "kb_s", "vb_s", "cv_s")}
    for layer in range(d.depth):
        j = layer // 2
        n = rmsnorm_bf16(h, norm_mix, layer)
        if layer % 2 == 0:
            z = matmul(n, w_in_a, j)
            q, kf, kb, vb, qi, kif, kib, wi = dsa_post(z, q_norm_a, k_norm_a, j, tabs, d)
            o = dsa_prompt_attention(q, kb, vb, qi, kib, wi, d)
            qi_rows = jnp.transpose(qi[:, d.NP + d.soff: d.NP + d.soff + Bd * Ts], (1, 0, 2))
            qi_s = qi_rows.reshape(Bd, Ts * d.nih, idim)
            wi_s = _sample_rows(wi, d, d.nih).reshape(Bd, Ts * d.nih, 1)
            scores = sample_idx_scores(page_table, qi_s, wi_s, cache_kidx_a, j, kib, d)
            sel = sample_select(scores, d)
            o_s = sample_dsa_attention(page_table, _sample_rows(q, d, d.n_heads), sel,
                                       ck_a, cv_a, j, kb, vb, d)
            o = _with_sample_rows(o, o_s, d)
            vf = z[:, d.QW + d.KVW: d.QW + 2 * d.KVW]
            for name, rows in (("ka", kf), ("va", vf), ("ki", kif)):
                p_, s_ = _to_seq(rows, d)
                outs[name + "_p"].append(p_)
                outs[name + "_s"].append(s_)
            w_out = w_out_a
        else:
            z, zb = matmul(n, w_in_b, j, emit_bf16=True)
            o = sb_prompt_attention(zb, d.n_heads, d.n_heads + d.n_kv, d)
            o_s = sample_sb_attention(page_table, _sample_rows(zb[:, :d.QW], d, d.n_heads),
                                      ck_b, cv_b, j, zb, d.QW // d.KVW, d.QW // d.KVW + 1, d)
            o = _with_sample_rows(o, o_s, d)
            for name, lo in (("kb", d.QW), ("vb", d.QW + d.KVW)):
                p_, s_ = _to_seq(z[:, lo: lo + d.KVW], d)
                outs[name + "_p"].append(p_)
                outs[name + "_s"].append(s_)
            w_out = w_out_b
        h = matmul(o, w_out, j, resid=h)

        n = rmsnorm_bf16(h, norm_ffn, layer)
        gu = matmul(n, w_gate_up, layer)
        st = state_ffn_conv[layer]
        zrow = jnp.zeros((Bd, 1, d.d_ff), F32)
        pad_s = jnp.zeros((Bd, Ts - 2, d.d_ff), F32)
        p1 = jnp.concatenate([st[:, 1:2], zrow, pad_s], axis=1).reshape(Bd * Ts, d.d_ff)
        p2 = jnp.concatenate([st[:, 0:1], st[:, 1:2], pad_s], axis=1).reshape(Bd * Ts, d.d_ff)
        top = jnp.zeros((d.soff, d.d_ff), F32)
        bot = jnp.zeros((npad, d.d_ff), F32)
        prev1 = jnp.concatenate([top, p1, bot], axis=0)
        prev2 = jnp.concatenate([top, p2, bot], axis=0)
        act = conv_glu_gate(gu, conv_w, conv_b, layer, prev1, prev2, d)
        h = matmul(act, w_down, layer, k_block=0, k_blocks=2, resid=h)
        h = matmul(act, w_down, layer, k_block=1, k_blocks=2, resid=h)
        g_p, g_s = _to_seq(gu[:, :d.d_ff], d)
        outs["cv_p"].append(g_p[:, -2:])
        outs["cv_s"].append(g_s[:, -2:])

    def heads(xs):
        a = jnp.stack(xs)
        return a.reshape(a.shape[:3] + (d.n_kv, d.hd))

    y_p = h[: d.NP].reshape(B, SEQ, D)
    y_s = h[d.NP + d.soff: d.NP + d.soff + Bd * Ts].reshape(Bd, Ts, D)
    return (y_p, y_s,
            heads(outs["ka_p"]), heads(outs["va_p"]), jnp.stack(outs["ki_p"]),
            heads(outs["kb_p"]), heads(outs["vb_p"]), jnp.stack(outs["cv_p"]),
            heads(outs["ka_s"]), heads(outs["va_s"]), jnp.stack(outs["ki_s"]),
            heads(outs["kb_s"]), heads(outs["vb_s"]), jnp.stack(outs["cv_s"]))
```

```python
import functools
from typing import NamedTuple

import jax
import jax.numpy as jnp
from jax import lax
from jax.experimental import pallas as pl
from jax.experimental.pallas import tpu as pltpu

F32 = jnp.float32
BF16 = jnp.bfloat16
I32 = jnp.int32

TOPK_MAX = 256
ROPE_THETA = 500000.0
EPS = 1e-6
LANE = 128
SUBLANE = 8
TAIL = 128
NEG = -1e30
INT_MIN = -2 ** 31
VMEM_LIMIT = 56 * 1024 * 1024


class Dims(NamedTuple):
    B: int
    SEQ: int
    D: int
    Bd: int
    Ts: int
    n_meta: int
    page: int
    n_pages: int
    n_kv: int
    hd: int
    n_heads: int
    nih: int
    idim: int
    d_ff: int
    depth: int

    @property
    def NP(self):
        return self.B * self.SEQ

    @property
    def M(self):
        return self.NP + TAIL

    @property
    def QW(self):
        return self.n_heads * self.hd

    @property
    def KVW(self):
        return self.n_kv * self.hd

    @property
    def IQW(self):
        return self.nih * self.idim

    @property
    def G(self):
        return self.n_heads // self.n_kv

    @property
    def soff(self):
        return self.B * self.n_meta

    @property
    def past(self):
        return self.n_pages * self.page

    @property
    def nqb(self):
        return self.SEQ // TAIL

    @property
    def scale(self):
        return self.hd ** -0.5


def _cparams(n_axes):
    return pltpu.CompilerParams(dimension_semantics=("arbitrary",) * n_axes,
                                vmem_limit_bytes=VMEM_LIMIT)


def _pick_tile(total, target, align):
    best = None
    t = align
    while t <= min(total, target):
        if total % t == 0:
            best = t
        t += align
    assert best is not None, (total, target, align)
    return best


def _div(x, n):
    if n & (n - 1) == 0:
        return x >> (n.bit_length() - 1)
    return x // n


def _mod(x, n):
    if n & (n - 1) == 0:
        return x & (n - 1)
    return x % n


def _dot_nt(a, b):
    return lax.dot_general(a, b, (((1,), (1,)), ((), ())), preferred_element_type=F32)


def _dot(a, b):
    return jnp.dot(a, b, preferred_element_type=F32)


def _rmsnorm_kernel(x_ref, g_ref, o_ref):
    x = x_ref[...]
    ms = jnp.mean(x * x, axis=-1, keepdims=True)
    o_ref[...] = (x * lax.rsqrt(ms + EPS) * g_ref[...]).astype(o_ref.dtype)


def rmsnorm_bf16(h, gains, layer):
    M, D = h.shape
    tr = _pick_tile(M, 256, 16)
    return pl.pallas_call(
        _rmsnorm_kernel,
        grid=(M // tr,),
        in_specs=[pl.BlockSpec((tr, D), lambda m: (m, 0)),
                  pl.BlockSpec((None, 1, D), lambda m: (layer, 0, 0))],
        out_specs=pl.BlockSpec((tr, D), lambda m: (m, 0)),
        out_shape=jax.ShapeDtypeStruct((M, D), BF16),
        compiler_params=_cparams(1),
    )(h, gains[:, None, :])


def _matmul_kernel(*refs, has_resid, emit_bf16):
    x_ref, w_ref = refs[0], refs[1]
    pos = 2
    r_ref = None
    if has_resid:
        r_ref = refs[pos]
        pos += 1
    o_ref = refs[pos]
    pos += 1
    ob_ref = None
    if emit_bf16:
        ob_ref = refs[pos]
        pos += 1
    wb_ref = refs[pos]

    @pl.when(pl.program_id(1) == 0)
    def _():
        wb_ref[...] = w_ref[...].astype(BF16)

    acc = _dot(x_ref[...], wb_ref[...])
    if has_resid:
        acc = acc + r_ref[...]
    o_ref[...] = acc
    if emit_bf16:
        ob_ref[...] = acc.astype(BF16)


def matmul(x, w, layer, *, k_block=0, k_blocks=1, resid=None, emit_bf16=False):
    M, Kx = x.shape
    _, Kw, N = w.shape
    assert Kx == Kw and Kx % k_blocks == 0
    K = Kx // k_blocks
    assert K % LANE == 0
    tm = _pick_tile(M, 640, 16)
    tn_target = 512
    tn = tn_target
    for cand in (tn_target, 256, 128):
        if cand <= tn_target and N % cand == 0:
            tn = cand
            break
    grid = (pl.cdiv(N, tn), M // tm)
    in_specs = [pl.BlockSpec((tm, K), lambda n, m: (m, k_block)),
                pl.BlockSpec((None, K, tn), lambda n, m: (layer, k_block, n))]
    args = [x, w]
    if resid is not None:
        in_specs.append(pl.BlockSpec((tm, tn), lambda n, m: (m, n)))
        args.append(resid)
    out_specs = [pl.BlockSpec((tm, tn), lambda n, m: (m, n))]
    out_shape = [jax.ShapeDtypeStruct((M, N), F32)]
    if emit_bf16:
        out_specs.append(pl.BlockSpec((tm, tn), lambda n, m: (m, n)))
        out_shape.append(jax.ShapeDtypeStruct((M, N), BF16))
    res = pl.pallas_call(
        functools.partial(_matmul_kernel, has_resid=resid is not None, emit_bf16=emit_bf16),
        grid=grid,
        in_specs=in_specs,
        out_specs=out_specs,
        out_shape=out_shape,
        scratch_shapes=[pltpu.VMEM((K, tn), BF16)],
        compiler_params=_cparams(2),
    )(*args)
    return res if emit_bf16 else res[0]


def _rope(x, c, s1, s2, rot):
    half = rot // 2
    return x * c + pltpu.roll(x, half, 1) * s1 + pltpu.roll(x, LANE - half, 1) * s2


def _dsa_post_kernel(z_ref, qg_ref, kg_ref, c_ref, s1_ref, s2_ref,
                     q_ref, kf_ref, kb_ref, vb_ref, qi_ref, kif_ref, kib_ref, wi_ref, *, d):
    c, s1, s2 = c_ref[...], s1_ref[...], s2_ref[...]
    rot = d.hd // 4
    hd = d.hd

    def headnorm(x, g):
        ms = jnp.mean(x * x, axis=-1, keepdims=True)
        return x * lax.rsqrt(ms + EPS) * g

    qg, kg = qg_ref[...], kg_ref[...]
    for h in range(d.n_heads):
        x = z_ref[:, h * hd:(h + 1) * hd]
        q_ref[:, h * hd:(h + 1) * hd] = _rope(headnorm(x, qg), c, s1, s2, rot).astype(BF16)
    off = d.QW
    for h in range(d.n_kv):
        x = z_ref[:, off + h * hd: off + (h + 1) * hd]
        y = _rope(headnorm(x, kg), c, s1, s2, rot)
        kf_ref[:, h * hd:(h + 1) * hd] = y
        kb_ref[:, h * hd:(h + 1) * hd] = y.astype(BF16)
    off = d.QW + d.KVW
    vb_ref[...] = z_ref[:, off: off + d.KVW].astype(BF16)
    off = d.QW + 2 * d.KVW
    for h in range(d.nih):
        x = z_ref[:, off + h * d.idim: off + (h + 1) * d.idim]
        qi_ref[h] = _rope(x, c, s1, s2, rot).astype(BF16)
    off = d.QW + 2 * d.KVW + d.IQW
    y = _rope(z_ref[:, off: off + d.idim], c, s1, s2, rot)
    kif_ref[...] = y
    kib_ref[...] = y.astype(BF16)
    off = off + d.idim
    wi_ref[...] = z_ref[:, off: off + d.nih] * (d.nih ** -0.5) * (d.idim ** -0.5)


def dsa_post(z, q_gain, k_gain, layer, tabs, d):
    M = d.M
    tr = TAIL
    c, s1, s2 = tabs
    row = lambda w: pl.BlockSpec((tr, w), lambda m: (m, 0))
    gain = pl.BlockSpec((None, 1, d.hd), lambda m: (layer, 0, 0))
    return pl.pallas_call(
        functools.partial(_dsa_post_kernel, d=d),
        grid=(M // tr,),
        in_specs=[row(z.shape[1]), gain, gain, row(LANE), row(LANE), row(LANE)],
        out_specs=[row(d.QW), row(d.KVW), row(d.KVW), row(d.KVW),
                   pl.BlockSpec((d.nih, tr, d.idim), lambda m: (0, m, 0)),
                   row(d.idim), row(d.idim), row(d.nih)],
        out_shape=[jax.ShapeDtypeStruct((M, d.QW), BF16),
                   jax.ShapeDtypeStruct((M, d.KVW), F32),
                   jax.ShapeDtypeStruct((M, d.KVW), BF16),
                   jax.ShapeDtypeStruct((M, d.KVW), BF16),
                   jax.ShapeDtypeStruct((d.nih, M, d.idim), BF16),
                   jax.ShapeDtypeStruct((M, d.idim), F32),
                   jax.ShapeDtypeStruct((M, d.idim), BF16),
                   jax.ShapeDtypeStruct((M, d.nih), F32)],
        compiler_params=_cparams(1),
    )(z, q_gain[:, None, :], k_gain[:, None, :], c, s1, s2)


def _float_key(x):
    bits = lax.bitcast_convert_type(x, I32)
    return bits ^ ((bits >> 31) & jnp.int32(0x7FFFFFFF))


def _kth_largest_key(key, k):
    kf = jnp.float32(k)

    def count_ge(cand):
        return jnp.sum(jnp.where(key >= cand, 1.0, 0.0), axis=1, keepdims=True)

    t0 = jnp.where(count_ge(jnp.int32(0)) >= kf, jnp.int32(0), jnp.int32(INT_MIN))

    def body(it, t):
        cand = t | lax.shift_left(jnp.int32(1), jnp.int32(30) - it)
        return jnp.where(count_ge(cand) >= kf, cand, t)

    return lax.fori_loop(0, 31, body, t0)


def _stack_heads(x, n, w):
    return jnp.concatenate([x[:, j * w:(j + 1) * w] for j in range(n)], axis=0)


def _prompt_query_meta(s, d):
    is_tail = s >= d.B * d.nqb
    b = jnp.minimum(s // d.nqb, d.B - 1)
    i = s % d.nqb
    r = lax.broadcasted_iota(I32, (TAIL, 1), 0)
    q_seq = jnp.where(is_tail, jnp.where(r < d.soff, _div(r, d.n_meta), -1), b)
    q_pos = jnp.where(is_tail, _mod(r, d.n_meta), d.n_meta + i * TAIL + r)
    return is_tail, b, i, q_seq, q_pos


def _tail_key_meta(d):
    c = lax.broadcasted_iota(I32, (1, TAIL), 1)
    k_seq = jnp.where(c < d.soff, _div(c, d.n_meta), -2)
    k_pos = _mod(c, d.n_meta)
    return k_seq, k_pos


def _dsa_prompt_kernel(qi_ref, wi_ref, kip_ref, kit_ref, q_ref, kp_ref, kt_ref, vp_ref, vt_ref,
                       o_ref, score_ref, sel_ref, *, d, n_sel):
    s = pl.program_id(0)
    g = pl.program_id(1)
    SEQ = d.SEQ

    @pl.when(g == 0)
    def _():
        _, b, _, q_seq, q_pos = _prompt_query_meta(s, d)
        cp = lax.broadcasted_iota(I32, (1, SEQ), 1)
        valid_p = (q_seq == b) & (d.n_meta + cp <= q_pos)
        tk_seq, tk_pos = _tail_key_meta(d)
        valid_t = (tk_seq == q_seq) & (tk_pos <= q_pos)
        valid = jnp.concatenate([valid_p, valid_t], axis=1)

        score_ref[...] = jnp.zeros_like(score_ref)
        wi = wi_ref[...]
        lane = lax.broadcasted_iota(I32, wi.shape, 1)

        def head(h, carry):
            qh = qi_ref[h]
            logit = jnp.concatenate([_dot_nt(qh, kip_ref[...]), _dot_nt(qh, kit_ref[...])], axis=1)
            wcol = jnp.sum(jnp.where(lane == h, wi, 0.0), axis=1, keepdims=True)
            score_ref[...] += wcol * jnp.maximum(logit, 0.0)
            return carry

        lax.fori_loop(0, d.nih, head, 0)
        key = jnp.where(valid, _float_key(score_ref[...]), jnp.int32(INT_MIN))
        t = _kth_largest_key(key, n_sel)
        sel_ref[...] = jnp.where(valid & (key >= t), 1.0, 0.0)

    G = d.G
    qg = _stack_heads(q_ref[...], G, d.hd)
    sc = jnp.concatenate([_dot_nt(qg, kp_ref[...]), _dot_nt(qg, kt_ref[...])], axis=1) * d.scale
    sel = jnp.concatenate([sel_ref[...]] * G, axis=0) > 0.5
    sc = jnp.where(sel, sc, NEG)
    m = jnp.max(sc, axis=1, keepdims=True)
    p = jnp.exp(sc - m)
    l = jnp.sum(p, axis=1, keepdims=True)
    pb = p.astype(BF16)
    o = (_dot(pb[:, :SEQ], vp_ref[...]) + _dot(pb[:, SEQ:], vt_ref[...])) / l
    for j in range(G):
        o_ref[:, j * d.hd:(j + 1) * d.hd] = o[j * TAIL:(j + 1) * TAIL].astype(o_ref.dtype)


def dsa_prompt_attention(q, kb, vb, qi, kib, wi, d):
    n_sel = min(TOPK_MAX, (d.SEQ + d.n_meta) // 4)
    nblk = d.B * d.nqb + 1
    tailb = d.NP // TAIL
    gw = d.G * d.hd
    seq_of = lambda s: jnp.minimum(s // d.nqb, d.B - 1)
    in_specs = [
        pl.BlockSpec((d.nih, TAIL, d.idim), lambda s, g: (0, s, 0)),
        pl.BlockSpec((TAIL, d.nih), lambda s, g: (s, 0)),
        pl.BlockSpec((d.SEQ, d.idim), lambda s, g: (seq_of(s), 0)),
        pl.BlockSpec((TAIL, d.idim), lambda s, g: (tailb, 0)),
        pl.BlockSpec((TAIL, gw), lambda s, g: (s, g)),
        pl.BlockSpec((d.SEQ, d.hd), lambda s, g: (seq_of(s), g)),
        pl.BlockSpec((TAIL, d.hd), lambda s, g: (tailb, g)),
        pl.BlockSpec((d.SEQ, d.hd), lambda s, g: (seq_of(s), g)),
        pl.BlockSpec((TAIL, d.hd), lambda s, g: (tailb, g)),
    ]
    return pl.pallas_call(
        functools.partial(_dsa_prompt_kernel, d=d, n_sel=n_sel),
        grid=(nblk, d.n_kv),
        in_specs=in_specs,
        out_specs=pl.BlockSpec((TAIL, gw), lambda s, g: (s, g)),
        out_shape=jax.ShapeDtypeStruct((d.M, d.QW), BF16),
        scratch_shapes=[pltpu.VMEM((TAIL, d.SEQ + TAIL), F32),
                        pltpu.VMEM((TAIL, d.SEQ + TAIL), F32)],
        compiler_params=_cparams(2),
    )(qi, wi, kib, kib, q, kb, kb, vb, vb)


SB_UNDERFLOW = 104.0


def _log_sigmoid(z):
    return jnp.minimum(z, 0.0) - jnp.log(1.0 + jnp.exp(-jnp.abs(z)))


def _sb_weights(z, valid, carry):
    ls = _log_sigmoid(z)
    lr = ls - z
    if valid is not None:
        lr = jnp.where(valid, lr, 0.0)
    ci = lax.broadcasted_iota(I32, (LANE, LANE), 0)
    cj = lax.broadcasted_iota(I32, (LANE, LANE), 1)
    upper = jnp.where(ci > cj, 1.0, 0.0).astype(BF16)
    hi = lr.astype(BF16)
    lo = (lr - hi.astype(F32)).astype(BF16)
    sfx = _dot(hi, upper) + _dot(lo, upper)
    a = jnp.exp(ls + sfx + carry)
    if valid is not None:
        a = jnp.where(valid, a, 0.0)
    return a.astype(BF16), jnp.sum(lr, axis=1, keepdims=True)


def _sb_block(z, valid, vblk, carry):
    a, tot = _sb_weights(z, valid, carry)
    return _dot(a, vblk), tot


def _sb_prompt_kernel(q_ref, kp_ref, kt_ref, vp_ref, vt_ref, o_ref, acc_ref, run_ref, *, d):
    s = pl.program_id(0)
    G = d.G
    _, _, i, q_seq, q_pos = _prompt_query_meta(s, d)
    q_seq = jnp.concatenate([q_seq] * G, axis=0)
    q_pos = jnp.concatenate([q_pos] * G, axis=0)
    qg = _stack_heads(q_ref[...], G, d.hd)
    cl = lax.broadcasted_iota(I32, (1, TAIL), 1)

    def block(kb, valid):
        start = pl.multiple_of(kb * TAIL, TAIL)
        z = _dot_nt(qg, kp_ref[pl.ds(start, TAIL), :]) * d.scale
        return _sb_block(z, valid, vp_ref[pl.ds(start, TAIL), :], run_ref[...])

    o, tot = _sb_block(_dot_nt(qg, kp_ref[pl.ds(pl.multiple_of(i * TAIL, TAIL), TAIL), :]) * d.scale,
                       (d.n_meta + i * TAIL + cl) < q_pos,
                       vp_ref[pl.ds(pl.multiple_of(i * TAIL, TAIL), TAIL), :], 0.0)
    acc_ref[...] = o
    run_ref[...] = tot

    def cond(c):
        kb, top = c
        return (kb >= 0) & (top > -SB_UNDERFLOW)

    def body(c):
        kb, _ = c
        o, tot = block(kb, None)
        acc_ref[...] += o
        run = run_ref[...] + tot
        run_ref[...] = run
        return kb - 1, jnp.max(run)

    _, top = lax.while_loop(cond, body, (i - 1, jnp.max(tot)))

    @pl.when(top > -SB_UNDERFLOW)
    def _():
        tk_seq, tk_pos = _tail_key_meta(d)
        valid = (tk_seq == q_seq) & (tk_pos < q_pos)
        o, _ = _sb_block(_dot_nt(qg, kt_ref[...]) * d.scale, valid, vt_ref[...], run_ref[...])
        acc_ref[...] += o

    for j in range(G):
        o_ref[:, j * d.hd:(j + 1) * d.hd] = acc_ref[j * TAIL:(j + 1) * TAIL, :].astype(o_ref.dtype)


def sb_prompt_attention(qb, kb, vb, d):
    zb, k_off, v_off = qb, kb, vb
    nblk = d.B * d.nqb + 1
    tailb = d.NP // TAIL
    gw = d.G * d.hd
    seq_of = lambda s: jnp.minimum(s // d.nqb, d.B - 1)
    in_specs = [
        pl.BlockSpec((TAIL, gw), lambda s, g: (s, g)),
        pl.BlockSpec((d.SEQ, d.hd), lambda s, g: (seq_of(s), k_off + g)),
        pl.BlockSpec((TAIL, d.hd), lambda s, g: (tailb, k_off + g)),
        pl.BlockSpec((d.SEQ, d.hd), lambda s, g: (seq_of(s), v_off + g)),
        pl.BlockSpec((TAIL, d.hd), lambda s, g: (tailb, v_off + g)),
    ]
    return pl.pallas_call(
        functools.partial(_sb_prompt_kernel, d=d),
        grid=(nblk, d.n_kv),
        in_specs=in_specs,
        out_specs=pl.BlockSpec((TAIL, gw), lambda s, g: (s, g)),
        out_shape=jax.ShapeDtypeStruct((d.M, d.QW), BF16),
        scratch_shapes=[pltpu.VMEM((d.G * TAIL, d.hd), F32), pltpu.VMEM((d.G * TAIL, 1), F32)],
        compiler_params=_cparams(2),
    )(zb, zb, zb, zb, zb)


def _block_diag_q(q, d):
    R = q.shape[0]
    rho = lax.broadcasted_iota(I32, (R, 1), 0)
    grp = _div(_mod(rho, d.n_heads), d.G)
    zero = jnp.zeros_like(q)
    return jnp.concatenate([jnp.where(grp == g, q, zero) for g in range(d.n_kv)], axis=1)


def _own_group_slab(oall, d):
    R = oall.shape[0]
    rho = lax.broadcasted_iota(I32, (R, 1), 0)
    grp = _div(_mod(rho, d.n_heads), d.G)
    out = jnp.zeros((R, d.hd), F32)
    for g in range(d.n_kv):
        out = out + jnp.where(grp == g, oall[:, g * d.hd:(g + 1) * d.hd], 0.0)
    return out


def _page_rows(page_ref, d):
    return jnp.concatenate([page_ref[:, h, :] for h in range(d.n_kv)], axis=1).astype(BF16)


IDX_PAGES_PER_STEP = 8
ATTN_PAGES_PER_STEP = 4


def _pages_per_step(n_pages, target):
    return max(g for g in range(1, target + 1) if n_pages % g == 0)


def _page_index_map(layer, j, pg, npg, trailing):
    def index_map(b, p, pt):
        return (layer, pt[b, jnp.minimum(p * pg + j, npg - 1)]) + (0,) * trailing
    return index_map


def _sample_idx_kernel(pt_ref, qi_ref, wi_ref, *refs, d, pg):
    kc_refs, kt_ref, o_ref = refs[:pg], refs[pg], refs[pg + 1]
    p = pl.program_id(1)
    last = d.n_pages // pg

    def scores(kblk):
        w = jnp.maximum(_dot_nt(qi_ref[...], kblk), 0.0) * wi_ref[...]
        rows = [jnp.sum(w[t * d.nih:(t + 1) * d.nih], axis=0, keepdims=True) for t in range(d.Ts)]
        rows.append(jnp.zeros((SUBLANE - d.Ts, kblk.shape[0]), F32))
        return jnp.concatenate(rows, axis=0)

    @pl.when(p < last)
    def _():
        o_ref[...] = scores(jnp.concatenate([r[...].astype(BF16) for r in kc_refs], axis=0))

    @pl.when(p == last)
    def _():
        o_ref[...] = jnp.zeros_like(o_ref)
        o_ref[:, :TAIL] = scores(kt_ref[...])


def sample_idx_scores(page_table, qi_s, wi_s, cache_ki, layer, kib, d):
    R = d.Ts * d.nih
    npg = d.n_pages
    pg = _pages_per_step(npg, IDX_PAGES_PER_STEP)
    tailb = d.NP // TAIL
    grid_spec = pltpu.PrefetchScalarGridSpec(
        num_scalar_prefetch=1,
        grid=(d.Bd, npg // pg + 1),
        in_specs=[
            pl.BlockSpec((None, R, d.idim), lambda b, p, pt: (b, 0, 0)),
            pl.BlockSpec((None, R, 1), lambda b, p, pt: (b, 0, 0)),
        ] + [
            pl.BlockSpec((None, None, d.page, d.idim), _page_index_map(layer, j, pg, npg, 2))
            for j in range(pg)
        ] + [
            pl.BlockSpec((TAIL, d.idim), lambda b, p, pt: (tailb, 0)),
        ],
        out_specs=pl.BlockSpec((None, SUBLANE, pg * LANE), lambda b, p, pt: (b, 0, p)),
    )
    return pl.pallas_call(
        functools.partial(_sample_idx_kernel, d=d, pg=pg),
        grid_spec=grid_spec,
        out_shape=jax.ShapeDtypeStruct((d.Bd, SUBLANE, (npg + pg) * LANE), F32),
        compiler_params=_cparams(2),
    )(page_table, qi_s, wi_s, *([cache_ki] * pg), kib)


def _sample_select_kernel(s_ref, o_ref, *, d, n_sel):
    b = pl.program_id(0)
    L = s_ref.shape[1]
    col = lax.broadcasted_iota(I32, (1, L), 1)
    t = lax.broadcasted_iota(I32, (SUBLANE, 1), 0)
    snew = col - d.past - d.soff - d.Ts * b
    valid = (col < d.past) | ((snew >= 0) & (snew < d.Ts) & (snew <= t))
    valid = valid & (t < d.Ts)
    key = jnp.where(valid, _float_key(s_ref[...]), jnp.int32(INT_MIN))
    thr = _kth_largest_key(key, n_sel)
    o_ref[...] = jnp.where(valid & (key >= thr), 1.0, 0.0)


def sample_select(scores, d):
    n_sel = min(TOPK_MAX, (d.past + d.Ts) // 4)
    L = scores.shape[2]
    spec = pl.BlockSpec((None, SUBLANE, L), lambda b: (b, 0, 0))
    return pl.pallas_call(
        functools.partial(_sample_select_kernel, d=d, n_sel=n_sel),
        grid=(d.Bd,),
        in_specs=[spec],
        out_specs=spec,
        out_shape=jax.ShapeDtypeStruct(scores.shape, F32),
        compiler_params=_cparams(1),
    )(scores)


def _sample_dsa_attn_kernel(pt_ref, q_ref, sel_ref, *refs, d, pg):
    kc_refs, vc_refs = refs[:pg], refs[pg:2 * pg]
    kt_ref, vt_ref, o_ref, expand_ref, own_ref, m_ref, l_ref, acc_ref = refs[2 * pg:]
    p = pl.program_id(1)
    last = d.n_pages // pg
    R = q_ref.shape[0]
    W = d.page * d.n_kv

    @pl.when(p == 0)
    def _():
        key = lax.broadcasted_iota(I32, (d.page, W), 0)
        col = lax.broadcasted_iota(I32, (d.page, W), 1)
        expand_ref[...] = jnp.where(_div(col, d.n_kv) == key, 1.0, 0.0).astype(BF16)
        rho = lax.broadcasted_iota(I32, (R, W), 0)
        col = lax.broadcasted_iota(I32, (R, W), 1)
        own_ref[...] = jnp.where(_mod(col, d.n_kv) == _div(_mod(rho, d.n_heads), d.G), 1.0, 0.0)
        m_ref[...] = jnp.full_like(m_ref, NEG)
        l_ref[...] = jnp.zeros_like(l_ref)
        acc_ref[...] = jnp.zeros_like(acc_ref)

    def per_row(sel):
        return jnp.concatenate(
            [jnp.broadcast_to(sel[t:t + 1], (d.n_heads, sel.shape[1])) for t in range(d.Ts)], axis=0)

    def update(sc, mask, pv):
        sc = jnp.where(mask, sc, NEG)
        m_old = m_ref[...]
        m_new = jnp.maximum(m_old, jnp.max(sc, axis=1, keepdims=True))
        alpha = jnp.exp(m_old - m_new)
        pr = jnp.where(mask, jnp.exp(sc - m_new), 0.0)
        l_ref[...] = alpha * l_ref[...] + jnp.sum(pr, axis=1, keepdims=True)
        acc_ref[...] = alpha * acc_ref[...] + pv(pr.astype(BF16))
        m_ref[...] = m_new

    @pl.when(p < last)
    def _():
        kflat = jnp.concatenate([r[...].reshape(W, d.hd) for r in kc_refs], axis=0).astype(BF16)
        vflat = jnp.concatenate([r[...].reshape(W, d.hd) for r in vc_refs], axis=0).astype(BF16)
        sc = _dot_nt(q_ref[...], kflat) * d.scale
        sel = sel_ref[...].astype(BF16)
        selx = jnp.concatenate([_dot(sel[:, j * d.page:(j + 1) * d.page], expand_ref[...])
                                for j in range(pg)], axis=1)
        own = jnp.concatenate([own_ref[...]] * pg, axis=1)
        update(sc, per_row(selx) * own > 0.5, lambda pr: _dot(pr, vflat))

    @pl.when(p == last)
    def _():
        sc = _dot_nt(_block_diag_q(q_ref[...], d), kt_ref[...]) * d.scale
        update(sc, per_row(sel_ref[:, :TAIL]) > 0.5,
               lambda pr: _own_group_slab(_dot(pr, vt_ref[...]), d))
        o_ref[...] = (acc_ref[...] / l_ref[...]).astype(o_ref.dtype)


def sample_dsa_attention(page_table, q_s, sel, cache_k, cache_v, layer, kb, vb, d):
    R = d.Ts * d.n_heads
    npg = d.n_pages
    pg = _pages_per_step(npg, ATTN_PAGES_PER_STEP)
    assert sel.shape[2] >= (npg + pg) * LANE
    tailb = d.NP // TAIL
    page_specs = [pl.BlockSpec((None, None, d.page, d.n_kv, d.hd),
                               _page_index_map(layer, j, pg, npg, 3)) for j in range(pg)]
    tail_spec = pl.BlockSpec((TAIL, d.KVW), lambda b, p, pt: (tailb, 0))
    grid_spec = pltpu.PrefetchScalarGridSpec(
        num_scalar_prefetch=1,
        grid=(d.Bd, npg // pg + 1),
        in_specs=[
            pl.BlockSpec((None, R, d.hd), lambda b, p, pt: (b, 0, 0)),
            pl.BlockSpec((None, SUBLANE, pg * LANE), lambda b, p, pt: (b, 0, p)),
        ] + page_specs + page_specs + [tail_spec, tail_spec],
        out_specs=pl.BlockSpec((None, R, d.hd), lambda b, p, pt: (b, 0, 0)),
        scratch_shapes=[pltpu.VMEM((d.page, d.page * d.n_kv), BF16),
                        pltpu.VMEM((R, d.page * d.n_kv), F32), pltpu.VMEM((R, 1), F32),
                        pltpu.VMEM((R, 1), F32), pltpu.VMEM((R, d.hd), F32)],
    )
    return pl.pallas_call(
        functools.partial(_sample_dsa_attn_kernel, d=d, pg=pg),
        grid_spec=grid_spec,
        out_shape=jax.ShapeDtypeStruct((d.Bd, R, d.hd), BF16),
        compiler_params=_cparams(2),
    )(page_table, q_s, sel, *([cache_k] * pg), *([cache_v] * pg), kb, vb)


def _sample_sb_kernel(pt_ref, q_ref, kt_ref, vt_ref, kc_hbm, vc_hbm, o_ref,
                      kbuf, vbuf, sems, *, d, layer):
    b = pl.program_id(0)
    R = q_ref.shape[0]
    qbd = _block_diag_q(q_ref[...], d)

    def block(kblk, vblk, valid, run):
        a, tot = _sb_weights(_dot_nt(qbd, kblk) * d.scale, valid, run)
        return _own_group_slab(_dot(a, vblk), d), tot

    rho = lax.broadcasted_iota(I32, (R, 1), 0)
    tq = _div(rho, d.n_heads)
    c = lax.broadcasted_iota(I32, (1, TAIL), 1)
    snew = c - d.soff - d.Ts * b
    acc0, run0 = block(kt_ref[...], vt_ref[...], (snew >= 0) & (snew < tq), 0.0)

    def page_copies(page):
        return (pltpu.make_async_copy(kc_hbm.at[layer, page], kbuf, sems.at[0]),
                pltpu.make_async_copy(vc_hbm.at[layer, page], vbuf, sems.at[1]))

    def cond(carry):
        p, top, _, _ = carry
        return (p >= 0) & (top > -SB_UNDERFLOW)

    def body(carry):
        p, _, acc, run = carry
        copies = page_copies(pt_ref[b, p])
        for cp in copies:
            cp.start()
        for cp in copies:
            cp.wait()
        od, tot = block(_page_rows(kbuf, d), _page_rows(vbuf, d), None, run)
        run = run + tot
        return p - 1, jnp.max(run), acc + od, run

    _, _, acc, _ = lax.while_loop(cond, body,
                                  (jnp.int32(d.n_pages - 1), jnp.max(run0), acc0, run0))
    o_ref[...] = acc.astype(o_ref.dtype)


def sample_sb_attention(page_table, q_s, cache_k, cache_v, layer, zb, k_off, v_off, d):
    R = d.Ts * d.n_heads
    tailb = d.NP // TAIL
    page_shape = (d.page, d.n_kv, d.hd)
    grid_spec = pltpu.PrefetchScalarGridSpec(
        num_scalar_prefetch=1,
        grid=(d.Bd,),
        in_specs=[
            pl.BlockSpec((None, R, d.hd), lambda b, pt: (b, 0, 0)),
            pl.BlockSpec((TAIL, d.KVW), lambda b, pt: (tailb, k_off)),
            pl.BlockSpec((TAIL, d.KVW), lambda b, pt: (tailb, v_off)),
            pl.BlockSpec(memory_space=pl.ANY),
            pl.BlockSpec(memory_space=pl.ANY),
        ],
        out_specs=pl.BlockSpec((None, R, d.hd), lambda b, pt: (b, 0, 0)),
        scratch_shapes=[pltpu.VMEM(page_shape, F32), pltpu.VMEM(page_shape, F32),
                        pltpu.SemaphoreType.DMA((2,))],
    )
    return pl.pallas_call(
        functools.partial(_sample_sb_kernel, d=d, layer=layer),
        grid_spec=grid_spec,
        out_shape=jax.ShapeDtypeStruct((d.Bd, R, d.hd), BF16),
        compiler_params=_cparams(1),
    )(page_table, q_s, zb, zb, cache_k, cache_v)


def _conv_glu_kernel(g_ref, halo_ref, u_ref, w_ref, b_ref, p1_ref, p2_ref, o_ref, *, d):
    m = pl.program_id(1)
    is_tail = m == d.NP // TAIL
    g = g_ref[...]
    gext = jnp.concatenate([halo_ref[...], g], axis=0)
    g1 = pltpu.roll(gext, 1, 0)[SUBLANE:]
    g2 = pltpu.roll(gext, 2, 0)[SUBLANE:]
    r = lax.broadcasted_iota(I32, (TAIL, 1), 0)
    step = jnp.where(r < d.soff, _mod(r, d.n_meta), _mod(r - d.soff, d.Ts))
    g1 = jnp.where(is_tail & (step < 1), p1_ref[...], g1)
    g2 = jnp.where(is_tail & (step < 2), p2_ref[...], g2)
    gc = b_ref[...] + g2 * w_ref[0:1, :] + g1 * w_ref[1:2, :] + g * w_ref[2:3, :]
    o_ref[...] = (gc * jax.nn.sigmoid(gc) * u_ref[...]).astype(o_ref.dtype)


def conv_glu_gate(gu, conv_w, conv_b, layer, prev1, prev2, d):
    assert conv_w.shape[1] == 3
    M = d.M
    tf = _pick_tile(d.d_ff, 5504, LANE)
    nf = d.d_ff // tf
    per_seq = d.nqb
    tail_m = d.NP // TAIL

    def halo_idx(f, m):
        b = m // per_seq
        seq_start = (d.NP + b * d.n_meta + d.n_meta - SUBLANE) // SUBLANE
        normal = (TAIL // SUBLANE) * m - 1
        idx = jnp.where(m % per_seq == 0, seq_start, normal)
        return (jnp.where(m == tail_m, 0, idx), f)

    return pl.pallas_call(
        functools.partial(_conv_glu_kernel, d=d),
        grid=(nf, M // TAIL),
        in_specs=[
            pl.BlockSpec((TAIL, tf), lambda f, m: (m, f)),
            pl.BlockSpec((SUBLANE, tf), halo_idx),
            pl.BlockSpec((TAIL, tf), lambda f, m: (m, nf + f)),
            pl.BlockSpec((None, 3, tf), lambda f, m: (layer, 0, f)),
            pl.BlockSpec((None, 1, tf), lambda f, m: (layer, 0, f)),
            pl.BlockSpec((TAIL, tf), lambda f, m: (0, f)),
            pl.BlockSpec((TAIL, tf), lambda f, m: (0, f)),
        ],
        out_specs=pl.BlockSpec((TAIL, tf), lambda f, m: (m, f)),
        out_shape=jax.ShapeDtypeStruct((M, d.d_ff), BF16),
        compiler_params=_cparams(2),
    )(gu, gu, gu, conv_w, conv_b[:, None, :], prev1, prev2)


def _rope_tables(pos, hd):
    rot = hd // 4
    half = rot // 2
    inv = ROPE_THETA ** (-jnp.arange(half, dtype=F32) / half)
    ang = pos.astype(F32)[:, None] * inv[None, :]
    cos, sin = jnp.cos(ang), jnp.sin(ang)
    n = pos.shape[0]
    ones = jnp.ones((n, hd - rot), F32)
    zeros = jnp.zeros((n, hd - rot), F32)
    zh = jnp.zeros((n, half), F32)
    c = jnp.concatenate([cos, cos, ones], axis=1)
    s1 = jnp.concatenate([zh, sin, zeros], axis=1)
    s2 = jnp.concatenate([-sin, zh, zeros], axis=1)
    return c, s1, s2


def _to_seq(rows, d):
    W = rows.shape[1]
    meta = rows[d.NP: d.NP + d.soff].reshape(d.B, d.n_meta, W)
    prm = rows[: d.NP].reshape(d.B, d.SEQ, W)
    smp = rows[d.NP + d.soff: d.NP + d.soff + d.Bd * d.Ts].reshape(d.Bd, d.Ts, W)
    return jnp.concatenate([meta, prm], axis=1), smp


def _sample_rows(x, d, per_row):
    w = x.shape[1] // per_row
    lo = d.NP + d.soff
    return x[lo: lo + d.Bd * d.Ts].reshape(d.Bd, d.Ts * per_row, w)


def _with_sample_rows(o, o_s, d):
    return lax.dynamic_update_slice(o, o_s.reshape(d.Bd * d.Ts, d.QW), (d.NP + d.soff, 0))


def kernel(x_prompt, x_sample, cache_k_a, cache_v_a, cache_kidx_a, cache_k_b, cache_v_b,
           state_ffn_conv, page_table, meta_tokens, norm_mix, norm_ffn, w_in_a, q_norm_a,
           k_norm_a, w_out_a, w_in_b, w_out_b, w_gate_up, conv_w, conv_b, w_down):
    B, SEQ, D = x_prompt.shape
    Bd, Ts, _ = x_sample.shape
    n_meta = meta_tokens.shape[0]
    _, n_pool, page, n_kv, hd = cache_k_a.shape
    idim = cache_kidx_a.shape[-1]
    QW = w_out_a.shape[1]
    nih = (w_in_a.shape[2] - QW - 2 * n_kv * hd - idim) // idim
    d = Dims(B=B, SEQ=SEQ, D=D, Bd=Bd, Ts=Ts, n_meta=n_meta, page=page,
             n_pages=page_table.shape[1], n_kv=n_kv, hd=hd, n_heads=QW // hd, nih=nih,
             idim=idim, d_ff=w_down.shape[1], depth=norm_mix.shape[0])
    assert SEQ % TAIL == 0 and page == LANE and hd == LANE and idim == LANE
    assert d.soff + Bd * Ts <= TAIL and n_meta % SUBLANE == 0 and 2 <= Ts <= SUBLANE
    assert w_in_a.shape[2] == d.QW + 2 * d.KVW + d.IQW + idim + nih
    assert (Ts * d.nih) % SUBLANE == 0 and d.nih % SUBLANE == 0
    npad = TAIL - d.soff - Bd * Ts

    h = jnp.concatenate([x_prompt.reshape(d.NP, D), jnp.tile(meta_tokens, (B, 1)),
                         x_sample.reshape(Bd * Ts, D), jnp.zeros((npad, D), F32)], axis=0)
    pos = jnp.concatenate([jnp.tile(n_meta + jnp.arange(SEQ), B), jnp.tile(jnp.arange(n_meta), B),
                           jnp.tile(d.past + jnp.arange(Ts), Bd), jnp.zeros((npad,), I32)])
    tabs = _rope_tables(pos, hd)

    outs = {k: [] for k in ("ka_p", "va_p", "ki_p", "kb_p", "vb_p", "cv_p",
                            "ka_s", "va_s", "ki_s", "kb_s", "vb_s", "cv_s")}
    for layer in range(d.depth):
        j = layer // 2
        n = rmsnorm_bf16(h, norm_mix, layer)
        if layer % 2 == 0:
            z = matmul(n, w_in_a, j)
            q, kf, kb, vb, qi, kif, kib, wi = dsa_post(z, q_norm_a, k_norm_a, j, tabs, d)
            o = dsa_prompt_attention(q, kb, vb, qi, kib, wi, d)
            qi_rows = jnp.transpose(qi[:, d.NP + d.soff: d.NP + d.soff + Bd * Ts], (1, 0, 2))
            qi_s = qi_rows.reshape(Bd, Ts * d.nih, idim)
            wi_s = _sample_rows(wi, d, d.nih).reshape(Bd, Ts * d.nih, 1)
            scores = sample_idx_scores(page_table, qi_s, wi_s, cache_kidx_a, j, kib, d)
            sel = sample_select(scores, d)
            o_s = sample_dsa_attention(page_table, _sample_rows(q, d, d.n_heads), sel,
                                       cache_k_a, cache_v_a, j, kb, vb, d)
            o = _with_sample_rows(o, o_s, d)
            vf = z[:, d.QW + d.KVW: d.QW + 2 * d.KVW]
            for name, rows in (("ka", kf), ("va", vf), ("ki", kif)):
                p_, s_ = _to_seq(rows, d)
                outs[name + "_p"].append(p_)
                outs[name + "_s"].append(s_)
            w_out = w_out_a
        else:
            z, zb = matmul(n, w_in_b, j, emit_bf16=True)
            o = sb_prompt_attention(zb, d.n_heads, d.n_heads + d.n_kv, d)
            lo = d.NP + d.soff
            q_s = zb[lo: lo + Bd * Ts, :d.QW].reshape(Bd, Ts * d.n_heads, hd)
            o_s = sample_sb_attention(page_table, q_s, cache_k_b, cache_v_b, j, zb,
                                      d.QW // d.KVW, d.QW // d.KVW + 1, d)
            o = _with_sample_rows(o, o_s, d)
            for name, lo in (("kb", d.QW), ("vb", d.QW + d.KVW)):
                p_, s_ = _to_seq(z[:, lo: lo + d.KVW], d)
                outs[name + "_p"].append(p_)
                outs[name + "_s"].append(s_)
            w_out = w_out_b
        h = matmul(o, w_out, j, resid=h)

        n = rmsnorm_bf16(h, norm_ffn, layer)
        gu = matmul(n, w_gate_up, layer)
        st = state_ffn_conv[layer]
        zrow = jnp.zeros((Bd, 1, d.d_ff), F32)
        pad_s = jnp.zeros((Bd, Ts - 2, d.d_ff), F32)
        p1 = jnp.concatenate([st[:, 1:2], zrow, pad_s], axis=1).reshape(Bd * Ts, d.d_ff)
        p2 = jnp.concatenate([st[:, 0:1], st[:, 1:2], pad_s], axis=1).reshape(Bd * Ts, d.d_ff)
        top = jnp.zeros((d.soff, d.d_ff), F32)
        bot = jnp.zeros((npad, d.d_ff), F32)
        prev1 = jnp.concatenate([top, p1, bot], axis=0)
        prev2 = jnp.concatenate([top, p2, bot], axis=0)
        act = conv_glu_gate(gu, conv_w, conv_b, layer, prev1, prev2, d)
        h = matmul(act, w_down, layer, k_block=0, k_blocks=2, resid=h)
        h = matmul(act, w_down, layer, k_block=1, k_blocks=2, resid=h)
        lo = d.NP + d.soff
        outs["cv_p"].append(gu[: d.NP].reshape(B, SEQ, 2 * d.d_ff)[:, -2:, :d.d_ff])
        outs["cv_s"].append(gu[lo: lo + Bd * Ts].reshape(Bd, Ts, 2 * d.d_ff)[:, -2:, :d.d_ff])

    def heads(xs):
        a = jnp.stack(xs)
        return a.reshape(a.shape[:3] + (d.n_kv, d.hd))

    y_p = h[: d.NP].reshape(B, SEQ, D)
    y_s = h[d.NP + d.soff: d.NP + d.soff + Bd * Ts].reshape(Bd, Ts, D)
    return (y_p, y_s,
            heads(outs["ka_p"]), heads(outs["va_p"]), jnp.stack(outs["ki_p"]),
            heads(outs["kb_p"]), heads(outs["vb_p"]), jnp.stack(outs["cv_p"]),
            heads(outs["ka_s"]), heads(outs["va_s"]), jnp.stack(outs["ki_s"]),
            heads(outs["kb_s"]), heads(outs["vb_s"]), jnp.stack(outs["cv_s"]))
```

```python
import functools
from typing import NamedTuple

import jax
import jax.numpy as jnp
from jax import lax
from jax.experimental import pallas as pl
from jax.experimental.pallas import tpu as pltpu

F32 = jnp.float32
BF16 = jnp.bfloat16
I32 = jnp.int32

TOPK_MAX = 256
ROPE_THETA = 500000.0
EPS = 1e-6
LANE = 128
SUBLANE = 8
TAIL = 128
NEG = -1e30
INT_MIN = -2 ** 31
VMEM_LIMIT = 56 * 1024 * 1024


class Dims(NamedTuple):
    B: int
    SEQ: int
    D: int
    Bd: int
    Ts: int
    n_meta: int
    page: int
    n_pages: int
    n_kv: int
    hd: int
    n_heads: int
    nih: int
    idim: int
    d_ff: int
    depth: int

    @property
    def NP(self):
        return self.B * self.SEQ

    @property
    def M(self):
        return self.NP + TAIL

    @property
    def QW(self):
        return self.n_heads * self.hd

    @property
    def KVW(self):
        return self.n_kv * self.hd

    @property
    def IQW(self):
        return self.nih * self.idim

    @property
    def G(self):
        return self.n_heads // self.n_kv

    @property
    def soff(self):
        return self.B * self.n_meta

    @property
    def past(self):
        return self.n_pages * self.page

    @property
    def nqb(self):
        return self.SEQ // TAIL

    @property
    def scale(self):
        return self.hd ** -0.5


def _cparams(n_axes):
    return pltpu.CompilerParams(dimension_semantics=("arbitrary",) * n_axes,
                                vmem_limit_bytes=VMEM_LIMIT)


def _pick_tile(total, target, align):
    best = None
    t = align
    while t <= min(total, target):
        if total % t == 0:
            best = t
        t += align
    assert best is not None, (total, target, align)
    return best


def _div(x, n):
    if n & (n - 1) == 0:
        return x >> (n.bit_length() - 1)
    return x // n


def _mod(x, n):
    if n & (n - 1) == 0:
        return x & (n - 1)
    return x % n


def _dot_nt(a, b):
    return lax.dot_general(a, b, (((1,), (1,)), ((), ())), preferred_element_type=F32)


def _dot(a, b):
    return jnp.dot(a, b, preferred_element_type=F32)


def _rmsnorm_kernel(x_ref, g_ref, o_ref):
    x = x_ref[...]
    ms = jnp.mean(x * x, axis=-1, keepdims=True)
    o_ref[...] = (x * lax.rsqrt(ms + EPS) * g_ref[...]).astype(o_ref.dtype)


def rmsnorm_bf16(h, gains, layer):
    M, D = h.shape
    tr = _pick_tile(M, 256, 16)
    return pl.pallas_call(
        _rmsnorm_kernel,
        grid=(M // tr,),
        in_specs=[pl.BlockSpec((tr, D), lambda m: (m, 0)),
                  pl.BlockSpec((None, 1, D), lambda m: (layer, 0, 0))],
        out_specs=pl.BlockSpec((tr, D), lambda m: (m, 0)),
        out_shape=jax.ShapeDtypeStruct((M, D), BF16),
        compiler_params=_cparams(1),
    )(h, gains[:, None, :])


def _matmul_kernel(*refs, has_resid, emit_bf16):
    x_ref, w_ref = refs[0], refs[1]
    pos = 2
    r_ref = None
    if has_resid:
        r_ref = refs[pos]
        pos += 1
    o_ref = refs[pos]
    pos += 1
    ob_ref = None
    if emit_bf16:
        ob_ref = refs[pos]
        pos += 1
    wb_ref = refs[pos]

    @pl.when(pl.program_id(1) == 0)
    def _():
        wb_ref[...] = w_ref[...].astype(BF16)

    acc = _dot(x_ref[...], wb_ref[...])
    if has_resid:
        acc = acc + r_ref[...]
    o_ref[...] = acc
    if emit_bf16:
        ob_ref[...] = acc.astype(BF16)


def matmul(x, w, layer, *, k_block=0, k_blocks=1, resid=None, emit_bf16=False):
    M, Kx = x.shape
    _, Kw, N = w.shape
    assert Kx == Kw and Kx % k_blocks == 0
    K = Kx // k_blocks
    assert K % LANE == 0
    tm = _pick_tile(M, 1040 if K <= 4096 else 640, 16)
    tn_target = 512
    tn = tn_target
    for cand in (tn_target, 256, 128):
        if cand <= tn_target and N % cand == 0:
            tn = cand
            break
    grid = (pl.cdiv(N, tn), M // tm)
    in_specs = [pl.BlockSpec((tm, K), lambda n, m: (m, k_block)),
                pl.BlockSpec((None, K, tn), lambda n, m: (layer, k_block, n))]
    args = [x, w]
    if resid is not None:
        in_specs.append(pl.BlockSpec((tm, tn), lambda n, m: (m, n)))
        args.append(resid)
    out_specs = [pl.BlockSpec((tm, tn), lambda n, m: (m, n))]
    out_shape = [jax.ShapeDtypeStruct((M, N), F32)]
    if emit_bf16:
        out_specs.append(pl.BlockSpec((tm, tn), lambda n, m: (m, n)))
        out_shape.append(jax.ShapeDtypeStruct((M, N), BF16))
    res = pl.pallas_call(
        functools.partial(_matmul_kernel, has_resid=resid is not None, emit_bf16=emit_bf16),
        grid=grid,
        in_specs=in_specs,
        out_specs=out_specs,
        out_shape=out_shape,
        scratch_shapes=[pltpu.VMEM((K, tn), BF16)],
        compiler_params=_cparams(2),
    )(*args)
    return res if emit_bf16 else res[0]


def _rope(x, c, s1, s2, rot):
    half = rot // 2
    return x * c + pltpu.roll(x, half, 1) * s1 + pltpu.roll(x, LANE - half, 1) * s2


def _dsa_post_kernel(z_ref, qg_ref, kg_ref, c_ref, s1_ref, s2_ref,
                     q_ref, kf_ref, kb_ref, vb_ref, qi_ref, kif_ref, kib_ref, wi_ref,
                     kt_ref, kit_ref, *, d):
    c, s1, s2 = c_ref[...], s1_ref[...], s2_ref[...]
    rot = d.hd // 4
    hd = d.hd

    def headnorm(x, g):
        ms = jnp.mean(x * x, axis=-1, keepdims=True)
        return x * lax.rsqrt(ms + EPS) * g

    qg, kg = qg_ref[...], kg_ref[...]
    for h in range(d.n_heads):
        x = z_ref[:, h * hd:(h + 1) * hd]
        q_ref[:, h * hd:(h + 1) * hd] = _rope(headnorm(x, qg), c, s1, s2, rot).astype(BF16)
    off = d.QW
    for h in range(d.n_kv):
        x = z_ref[:, off + h * hd: off + (h + 1) * hd]
        y = _rope(headnorm(x, kg), c, s1, s2, rot)
        kf_ref[:, h * hd:(h + 1) * hd] = y
        kb_ref[:, h * hd:(h + 1) * hd] = y.astype(BF16)
        kt_ref[h * hd:(h + 1) * hd, :] = y.T.astype(BF16)
    off = d.QW + d.KVW
    vb_ref[...] = z_ref[:, off: off + d.KVW].astype(BF16)
    off = d.QW + 2 * d.KVW
    for h in range(d.nih):
        x = z_ref[:, off + h * d.idim: off + (h + 1) * d.idim]
        qi_ref[h] = _rope(x, c, s1, s2, rot).astype(BF16)
    off = d.QW + 2 * d.KVW + d.IQW
    y = _rope(z_ref[:, off: off + d.idim], c, s1, s2, rot)
    kif_ref[...] = y
    kib_ref[...] = y.astype(BF16)
    kit_ref[...] = y.T.astype(BF16)
    off = off + d.idim
    wi_ref[...] = z_ref[:, off: off + d.nih] * (d.nih ** -0.5) * (d.idim ** -0.5)


def dsa_post(z, q_gain, k_gain, layer, tabs, d):
    M = d.M
    tr = TAIL
    c, s1, s2 = tabs
    row = lambda w: pl.BlockSpec((tr, w), lambda m: (m, 0))
    gain = pl.BlockSpec((None, 1, d.hd), lambda m: (layer, 0, 0))
    return pl.pallas_call(
        functools.partial(_dsa_post_kernel, d=d),
        grid=(M // tr,),
        in_specs=[row(z.shape[1]), gain, gain, row(LANE), row(LANE), row(LANE)],
        out_specs=[row(d.QW), row(d.KVW), row(d.KVW), row(d.KVW),
                   pl.BlockSpec((d.nih, tr, d.idim), lambda m: (0, m, 0)),
                   row(d.idim), row(d.idim), row(d.nih),
                   pl.BlockSpec((d.KVW, tr), lambda m: (0, m)),
                   pl.BlockSpec((d.idim, tr), lambda m: (0, m))],
        out_shape=[jax.ShapeDtypeStruct((M, d.QW), BF16),
                   jax.ShapeDtypeStruct((M, d.KVW), F32),
                   jax.ShapeDtypeStruct((M, d.KVW), BF16),
                   jax.ShapeDtypeStruct((M, d.KVW), BF16),
                   jax.ShapeDtypeStruct((d.nih, M, d.idim), BF16),
                   jax.ShapeDtypeStruct((M, d.idim), F32),
                   jax.ShapeDtypeStruct((M, d.idim), BF16),
                   jax.ShapeDtypeStruct((M, d.nih), F32),
                   jax.ShapeDtypeStruct((d.KVW, M), BF16),
                   jax.ShapeDtypeStruct((d.idim, M), BF16)],
        compiler_params=_cparams(1),
    )(z, q_gain[:, None, :], k_gain[:, None, :], c, s1, s2)


def _float_key(x):
    bits = lax.bitcast_convert_type(x, I32)
    return bits ^ ((bits >> 31) & jnp.int32(0x7FFFFFFF))


def _kth_largest_key(key, k):
    kf = jnp.float32(k)

    def count_ge(cand):
        return jnp.sum(jnp.where(key >= cand, 1.0, 0.0), axis=1, keepdims=True)

    t0 = jnp.where(count_ge(jnp.int32(0)) >= kf, jnp.int32(0), jnp.int32(INT_MIN))

    def body(it, t):
        cand = t | lax.shift_left(jnp.int32(1), jnp.int32(30) - it)
        return jnp.where(count_ge(cand) >= kf, cand, t)

    return lax.fori_loop(0, 31, body, t0)


def _stack_heads(x, n, w):
    return jnp.concatenate([x[:, j * w:(j + 1) * w] for j in range(n)], axis=0)


def _prompt_query_meta(s, d):
    is_tail = s >= d.B * d.nqb
    b = jnp.minimum(s // d.nqb, d.B - 1)
    i = s % d.nqb
    r = lax.broadcasted_iota(I32, (TAIL, 1), 0)
    q_seq = jnp.where(is_tail, jnp.where(r < d.soff, _div(r, d.n_meta), -1), b)
    q_pos = jnp.where(is_tail, _mod(r, d.n_meta), d.n_meta + i * TAIL + r)
    return is_tail, b, i, q_seq, q_pos


def _tail_key_meta(d):
    c = lax.broadcasted_iota(I32, (1, TAIL), 1)
    k_seq = jnp.where(c < d.soff, _div(c, d.n_meta), -2)
    k_pos = _mod(c, d.n_meta)
    return k_seq, k_pos


DSA_EXTENT_STEP = 4
IDX_KEY_CHUNK = 256


def _dsa_prompt_kernel(qi_ref, wi_ref, kip_ref, kit_ref, q_ref, kp_ref, kt_ref, vp_ref, vt_ref,
                       o_ref, score_ref, sel_ref, wb_ref, *, d, n_sel):
    s = pl.program_id(0)
    g = pl.program_id(1)
    _, b, i, q_seq, q_pos = _prompt_query_meta(s, d)

    def body(npk):
        W = npk + TAIL

        @pl.when(g == 0)
        def _():
            cp = lax.broadcasted_iota(I32, (1, npk), 1)
            valid_p = (q_seq == b) & (d.n_meta + cp <= q_pos)
            tk_seq, tk_pos = _tail_key_meta(d)
            valid_t = (tk_seq == q_seq) & (tk_pos <= q_pos)
            valid = jnp.concatenate([valid_p, valid_t], axis=1)

            wi = wi_ref[...]
            for h in range(d.nih):
                wb_ref[h * TAIL:(h + 1) * TAIL, :] = jnp.broadcast_to(wi[:, h:h + 1], (TAIL, LANE))
            qall = qi_ref[...].reshape(d.nih * TAIL, d.idim)

            def chunk_scores(kchunk_t):
                n = kchunk_t.shape[1]
                w = jnp.maximum(_dot(qall, kchunk_t), 0.0)
                w = w * jnp.concatenate([wb_ref[...]] * (n // LANE), axis=1)
                return jnp.sum(w.reshape(d.nih, TAIL, n), axis=0)

            for c0 in range(0, npk, IDX_KEY_CHUNK):
                c1 = min(npk, c0 + IDX_KEY_CHUNK)
                score_ref[:, c0:c1] = chunk_scores(kip_ref[:, c0:c1])
            score_ref[:, npk:W] = chunk_scores(kit_ref[...])
            key = jnp.where(valid, _float_key(score_ref[:, :W]), jnp.int32(INT_MIN))
            t = _kth_largest_key(key, n_sel)
            sel_ref[:, :W] = jnp.where(valid & (key >= t), 1.0, 0.0)

        G = d.G
        qg = _stack_heads(q_ref[...], G, d.hd)
        sc = jnp.concatenate(
            [_dot(qg, kp_ref[:, :npk]), _dot(qg, kt_ref[...])], axis=1) * d.scale
        sel = jnp.concatenate([sel_ref[:, :W]] * G, axis=0) > 0.5
        sc = jnp.where(sel, sc, NEG)
        m = jnp.max(sc, axis=1, keepdims=True)
        p = jnp.exp(sc - m)
        l = jnp.sum(p, axis=1, keepdims=True)
        pb = p.astype(BF16)
        o = (_dot(pb[:, :npk], vp_ref[:npk, :]) + _dot(pb[:, npk:], vt_ref[...])) / l
        for j in range(G):
            o_ref[:, j * d.hd:(j + 1) * d.hd] = o[j * TAIL:(j + 1) * TAIL].astype(o_ref.dtype)

    extents = sorted({min(d.nqb, e) for e in range(DSA_EXTENT_STEP, d.nqb + DSA_EXTENT_STEP,
                                                    DSA_EXTENT_STEP)})
    for lo, ext in zip([0] + extents[:-1], extents):
        pl.when((i >= lo) & (i < ext))(functools.partial(body, ext * TAIL))


def dsa_prompt_attention(q, kt, vb, qi, kit, wi, d):
    n_sel = min(TOPK_MAX, (d.SEQ + d.n_meta) // 4)
    nblk = d.B * d.nqb + 1
    tailb = d.NP // TAIL
    gw = d.G * d.hd
    seq_of = lambda s: jnp.minimum(s // d.nqb, d.B - 1)
    in_specs = [
        pl.BlockSpec((d.nih, TAIL, d.idim), lambda s, g: (0, s, 0)),
        pl.BlockSpec((TAIL, d.nih), lambda s, g: (s, 0)),
        pl.BlockSpec((d.idim, d.SEQ), lambda s, g: (0, seq_of(s))),
        pl.BlockSpec((d.idim, TAIL), lambda s, g: (0, tailb)),
        pl.BlockSpec((TAIL, gw), lambda s, g: (s, g)),
        pl.BlockSpec((d.hd, d.SEQ), lambda s, g: (g, seq_of(s))),
        pl.BlockSpec((d.hd, TAIL), lambda s, g: (g, tailb)),
        pl.BlockSpec((d.SEQ, d.hd), lambda s, g: (seq_of(s), g)),
        pl.BlockSpec((TAIL, d.hd), lambda s, g: (tailb, g)),
    ]
    return pl.pallas_call(
        functools.partial(_dsa_prompt_kernel, d=d, n_sel=n_sel),
        grid=(nblk, d.n_kv),
        in_specs=in_specs,
        out_specs=pl.BlockSpec((TAIL, gw), lambda s, g: (s, g)),
        out_shape=jax.ShapeDtypeStruct((d.M, d.QW), BF16),
        scratch_shapes=[pltpu.VMEM((TAIL, d.SEQ + TAIL), F32),
                        pltpu.VMEM((TAIL, d.SEQ + TAIL), F32),
                        pltpu.VMEM((d.nih * TAIL, LANE), F32)],
        compiler_params=_cparams(2),
    )(qi, wi, kit, kit, q, kt, kt, vb, vb)


SB_UNDERFLOW = 104.0


def _log_sigmoid(z):
    return jnp.minimum(z, 0.0) - jnp.log(1.0 + jnp.exp(-jnp.abs(z)))


def _sb_weights(z, valid, carry):
    ls = _log_sigmoid(z)
    lr = ls - z
    if valid is not None:
        lr = jnp.where(valid, lr, 0.0)
    ci = lax.broadcasted_iota(I32, (LANE, LANE), 0)
    cj = lax.broadcasted_iota(I32, (LANE, LANE), 1)
    upper = jnp.where(ci > cj, 1.0, 0.0).astype(BF16)
    hi = lr.astype(BF16)
    lo = (lr - hi.astype(F32)).astype(BF16)
    sfx = _dot(hi, upper) + _dot(lo, upper)
    a = jnp.exp(ls + sfx + carry)
    if valid is not None:
        a = jnp.where(valid, a, 0.0)
    return a.astype(BF16), jnp.sum(lr, axis=1, keepdims=True)


def _sb_block(z, valid, vblk, carry):
    a, tot = _sb_weights(z, valid, carry)
    return _dot(a, vblk), tot


def _sb_prompt_kernel(q_ref, kp_ref, kt_ref, vp_ref, vt_ref, o_ref, acc_ref, run_ref, *, d):
    s = pl.program_id(0)
    G = d.G
    _, _, i, q_seq, q_pos = _prompt_query_meta(s, d)
    q_seq = jnp.concatenate([q_seq] * G, axis=0)
    q_pos = jnp.concatenate([q_pos] * G, axis=0)
    qg = _stack_heads(q_ref[...], G, d.hd)
    cl = lax.broadcasted_iota(I32, (1, TAIL), 1)

    def block(kb, valid):
        start = pl.multiple_of(kb * TAIL, TAIL)
        z = _dot_nt(qg, kp_ref[pl.ds(start, TAIL), :]) * d.scale
        return _sb_block(z, valid, vp_ref[pl.ds(start, TAIL), :], run_ref[...])

    o, tot = _sb_block(_dot_nt(qg, kp_ref[pl.ds(pl.multiple_of(i * TAIL, TAIL), TAIL), :]) * d.scale,
                       (d.n_meta + i * TAIL + cl) < q_pos,
                       vp_ref[pl.ds(pl.multiple_of(i * TAIL, TAIL), TAIL), :], 0.0)
    acc_ref[...] = o
    run_ref[...] = tot

    def cond(c):
        kb, top = c
        return (kb >= 0) & (top > -SB_UNDERFLOW)

    def body(c):
        kb, _ = c
        o, tot = block(kb, None)
        acc_ref[...] += o
        run = run_ref[...] + tot
        run_ref[...] = run
        return kb - 1, jnp.max(run)

    _, top = lax.while_loop(cond, body, (i - 1, jnp.max(tot)))

    @pl.when(top > -SB_UNDERFLOW)
    def _():
        tk_seq, tk_pos = _tail_key_meta(d)
        valid = (tk_seq == q_seq) & (tk_pos < q_pos)
        o, _ = _sb_block(_dot_nt(qg, kt_ref[...]) * d.scale, valid, vt_ref[...], run_ref[...])
        acc_ref[...] += o

    for j in range(G):
        o_ref[:, j * d.hd:(j + 1) * d.hd] = acc_ref[j * TAIL:(j + 1) * TAIL, :].astype(o_ref.dtype)


def sb_prompt_attention(qb, kb, vb, d):
    zb, k_off, v_off = qb, kb, vb
    nblk = d.B * d.nqb + 1
    tailb = d.NP // TAIL
    gw = d.G * d.hd
    seq_of = lambda s: jnp.minimum(s // d.nqb, d.B - 1)
    in_specs = [
        pl.BlockSpec((TAIL, gw), lambda s, g: (s, g)),
        pl.BlockSpec((d.SEQ, d.hd), lambda s, g: (seq_of(s), k_off + g)),
        pl.BlockSpec((TAIL, d.hd), lambda s, g: (tailb, k_off + g)),
        pl.BlockSpec((d.SEQ, d.hd), lambda s, g: (seq_of(s), v_off + g)),
        pl.BlockSpec((TAIL, d.hd), lambda s, g: (tailb, v_off + g)),
    ]
    return pl.pallas_call(
        functools.partial(_sb_prompt_kernel, d=d),
        grid=(nblk, d.n_kv),
        in_specs=in_specs,
        out_specs=pl.BlockSpec((TAIL, gw), lambda s, g: (s, g)),
        out_shape=jax.ShapeDtypeStruct((d.M, d.QW), BF16),
        scratch_shapes=[pltpu.VMEM((d.G * TAIL, d.hd), F32), pltpu.VMEM((d.G * TAIL, 1), F32)],
        compiler_params=_cparams(2),
    )(zb, zb, zb, zb, zb)


def _block_diag_q(q, d):
    R = q.shape[0]
    rho = lax.broadcasted_iota(I32, (R, 1), 0)
    grp = _div(_mod(rho, d.n_heads), d.G)
    zero = jnp.zeros_like(q)
    return jnp.concatenate([jnp.where(grp == g, q, zero) for g in range(d.n_kv)], axis=1)


def _own_group_slab(oall, d):
    R = oall.shape[0]
    rho = lax.broadcasted_iota(I32, (R, 1), 0)
    grp = _div(_mod(rho, d.n_heads), d.G)
    out = jnp.zeros((R, d.hd), F32)
    for g in range(d.n_kv):
        out = out + jnp.where(grp == g, oall[:, g * d.hd:(g + 1) * d.hd], 0.0)
    return out


def _page_rows(page_ref, d):
    return jnp.concatenate([page_ref[:, h, :] for h in range(d.n_kv)], axis=1).astype(BF16)


IDX_PAGES_PER_STEP = 8
ATTN_PAGES_PER_STEP = 4


def _pages_per_step(n_pages, target):
    return max(g for g in range(1, target + 1) if n_pages % g == 0)


def _page_index_map(layer, j, pg, npg, trailing):
    def index_map(b, p, pt):
        return (layer, pt[b, jnp.minimum(p * pg + j, npg - 1)]) + (0,) * trailing
    return index_map


def _sample_idx_kernel(pt_ref, qi_ref, wi_ref, *refs, d, pg):
    kc_refs, kt_ref, o_ref = refs[:pg], refs[pg], refs[pg + 1]
    p = pl.program_id(1)
    last = d.n_pages // pg

    def scores(kblk):
        w = jnp.maximum(_dot_nt(qi_ref[...], kblk), 0.0) * wi_ref[...]
        rows = [jnp.sum(w[t * d.nih:(t + 1) * d.nih], axis=0, keepdims=True) for t in range(d.Ts)]
        rows.append(jnp.zeros((SUBLANE - d.Ts, kblk.shape[0]), F32))
        return jnp.concatenate(rows, axis=0)

    @pl.when(p < last)
    def _():
        o_ref[...] = scores(jnp.concatenate([r[...].astype(BF16) for r in kc_refs], axis=0))

    @pl.when(p == last)
    def _():
        o_ref[...] = jnp.zeros_like(o_ref)
        o_ref[:, :TAIL] = scores(kt_ref[...])


def sample_idx_scores(page_table, qi_s, wi_s, cache_ki, layer, kib, d):
    R = d.Ts * d.nih
    npg = d.n_pages
    pg = _pages_per_step(npg, IDX_PAGES_PER_STEP)
    tailb = d.NP // TAIL
    grid_spec = pltpu.PrefetchScalarGridSpec(
        num_scalar_prefetch=1,
        grid=(d.Bd, npg // pg + 1),
        in_specs=[
            pl.BlockSpec((None, R, d.idim), lambda b, p, pt: (b, 0, 0)),
            pl.BlockSpec((None, R, 1), lambda b, p, pt: (b, 0, 0)),
        ] + [
            pl.BlockSpec((None, None, d.page, d.idim), _page_index_map(layer, j, pg, npg, 2))
            for j in range(pg)
        ] + [
            pl.BlockSpec((TAIL, d.idim), lambda b, p, pt: (tailb, 0)),
        ],
        out_specs=pl.BlockSpec((None, SUBLANE, pg * LANE), lambda b, p, pt: (b, 0, p)),
    )
    return pl.pallas_call(
        functools.partial(_sample_idx_kernel, d=d, pg=pg),
        grid_spec=grid_spec,
        out_shape=jax.ShapeDtypeStruct((d.Bd, SUBLANE, (npg + pg) * LANE), F32),
        compiler_params=_cparams(2),
    )(page_table, qi_s, wi_s, *([cache_ki] * pg), kib)


def _sample_select_kernel(s_ref, o_ref, *, d, n_sel):
    b = pl.program_id(0)
    L = s_ref.shape[1]
    col = lax.broadcasted_iota(I32, (1, L), 1)
    t = lax.broadcasted_iota(I32, (SUBLANE, 1), 0)
    snew = col - d.past - d.soff - d.Ts * b
    valid = (col < d.past) | ((snew >= 0) & (snew < d.Ts) & (snew <= t))
    valid = valid & (t < d.Ts)
    key = jnp.where(valid, _float_key(s_ref[...]), jnp.int32(INT_MIN))
    thr = _kth_largest_key(key, n_sel)
    o_ref[...] = jnp.where(valid & (key >= thr), 1.0, 0.0)


def sample_select(scores, d):
    n_sel = min(TOPK_MAX, (d.past + d.Ts) // 4)
    L = scores.shape[2]
    spec = pl.BlockSpec((None, SUBLANE, L), lambda b: (b, 0, 0))
    return pl.pallas_call(
        functools.partial(_sample_select_kernel, d=d, n_sel=n_sel),
        grid=(d.Bd,),
        in_specs=[spec],
        out_specs=spec,
        out_shape=jax.ShapeDtypeStruct(scores.shape, F32),
        compiler_params=_cparams(1),
    )(scores)


def _sample_dsa_attn_kernel(pt_ref, q_ref, sel_ref, *refs, d, pg):
    kc_refs, vc_refs = refs[:pg], refs[pg:2 * pg]
    kt_ref, vt_ref, o_ref, expand_ref, own_ref, m_ref, l_ref, acc_ref = refs[2 * pg:]
    p = pl.program_id(1)
    last = d.n_pages // pg
    R = q_ref.shape[0]
    W = d.page * d.n_kv

    @pl.when(p == 0)
    def _():
        key = lax.broadcasted_iota(I32, (d.page, W), 0)
        col = lax.broadcasted_iota(I32, (d.page, W), 1)
        expand_ref[...] = jnp.where(_div(col, d.n_kv) == key, 1.0, 0.0).astype(BF16)
        rho = lax.broadcasted_iota(I32, (R, W), 0)
        col = lax.broadcasted_iota(I32, (R, W), 1)
        own_ref[...] = jnp.where(_mod(col, d.n_kv) == _div(_mod(rho, d.n_heads), d.G), 1.0, 0.0)
        m_ref[...] = jnp.full_like(m_ref, NEG)
        l_ref[...] = jnp.zeros_like(l_ref)
        acc_ref[...] = jnp.zeros_like(acc_ref)

    def per_row(sel):
        return jnp.concatenate(
            [jnp.broadcast_to(sel[t:t + 1], (d.n_heads, sel.shape[1])) for t in range(d.Ts)], axis=0)

    def update(sc, mask, pv):
        sc = jnp.where(mask, sc, NEG)
        m_old = m_ref[...]
        m_new = jnp.maximum(m_old, jnp.max(sc, axis=1, keepdims=True))
        alpha = jnp.exp(m_old - m_new)
        pr = jnp.where(mask, jnp.exp(sc - m_new), 0.0)
        l_ref[...] = alpha * l_ref[...] + jnp.sum(pr, axis=1, keepdims=True)
        acc_ref[...] = alpha * acc_ref[...] + pv(pr.astype(BF16))
        m_ref[...] = m_new

    @pl.when(p < last)
    def _():
        kflat = jnp.concatenate([r[...].reshape(W, d.hd) for r in kc_refs], axis=0).astype(BF16)
        vflat = jnp.concatenate([r[...].reshape(W, d.hd) for r in vc_refs], axis=0).astype(BF16)
        sc = _dot_nt(q_ref[...], kflat) * d.scale
        sel = sel_ref[...].astype(BF16)
        selx = jnp.concatenate([_dot(sel[:, j * d.page:(j + 1) * d.page], expand_ref[...])
                                for j in range(pg)], axis=1)
        own = jnp.concatenate([own_ref[...]] * pg, axis=1)
        update(sc, per_row(selx) * own > 0.5, lambda pr: _dot(pr, vflat))

    @pl.when(p == last)
    def _():
        sc = _dot_nt(_block_diag_q(q_ref[...], d), kt_ref[...]) * d.scale
        update(sc, per_row(sel_ref[:, :TAIL]) > 0.5,
               lambda pr: _own_group_slab(_dot(pr, vt_ref[...]), d))
        o_ref[...] = (acc_ref[...] / l_ref[...]).astype(o_ref.dtype)


def sample_dsa_attention(page_table, q_s, sel, cache_k, cache_v, layer, kb, vb, d):
    R = d.Ts * d.n_heads
    npg = d.n_pages
    pg = _pages_per_step(npg, ATTN_PAGES_PER_STEP)
    assert sel.shape[2] >= (npg + pg) * LANE
    tailb = d.NP // TAIL
    page_specs = [pl.BlockSpec((None, None, d.page, d.n_kv, d.hd),
                               _page_index_map(layer, j, pg, npg, 3)) for j in range(pg)]
    tail_spec = pl.BlockSpec((TAIL, d.KVW), lambda b, p, pt: (tailb, 0))
    grid_spec = pltpu.PrefetchScalarGridSpec(
        num_scalar_prefetch=1,
        grid=(d.Bd, npg // pg + 1),
        in_specs=[
            pl.BlockSpec((None, R, d.hd), lambda b, p, pt: (b, 0, 0)),
            pl.BlockSpec((None, SUBLANE, pg * LANE), lambda b, p, pt: (b, 0, p)),
        ] + page_specs + page_specs + [tail_spec, tail_spec],
        out_specs=pl.BlockSpec((None, R, d.hd), lambda b, p, pt: (b, 0, 0)),
        scratch_shapes=[pltpu.VMEM((d.page, d.page * d.n_kv), BF16),
                        pltpu.VMEM((R, d.page * d.n_kv), F32), pltpu.VMEM((R, 1), F32),
                        pltpu.VMEM((R, 1), F32), pltpu.VMEM((R, d.hd), F32)],
    )
    return pl.pallas_call(
        functools.partial(_sample_dsa_attn_kernel, d=d, pg=pg),
        grid_spec=grid_spec,
        out_shape=jax.ShapeDtypeStruct((d.Bd, R, d.hd), BF16),
        compiler_params=_cparams(2),
    )(page_table, q_s, sel, *([cache_k] * pg), *([cache_v] * pg), kb, vb)


def _sample_sb_kernel(pt_ref, q_ref, kt_ref, vt_ref, kc_hbm, vc_hbm, o_ref,
                      kbuf, vbuf, sems, *, d, layer):
    b = pl.program_id(0)
    R = q_ref.shape[0]
    qbd = _block_diag_q(q_ref[...], d)

    def block(kblk, vblk, valid, run):
        a, tot = _sb_weights(_dot_nt(qbd, kblk) * d.scale, valid, run)
        return _own_group_slab(_dot(a, vblk), d), tot

    rho = lax.broadcasted_iota(I32, (R, 1), 0)
    tq = _div(rho, d.n_heads)
    c = lax.broadcasted_iota(I32, (1, TAIL), 1)
    snew = c - d.soff - d.Ts * b
    acc0, run0 = block(kt_ref[...], vt_ref[...], (snew >= 0) & (snew < tq), 0.0)

    def page_copies(page):
        return (pltpu.make_async_copy(kc_hbm.at[layer, page], kbuf, sems.at[0]),
                pltpu.make_async_copy(vc_hbm.at[layer, page], vbuf, sems.at[1]))

    def cond(carry):
        p, top, _, _ = carry
        return (p >= 0) & (top > -SB_UNDERFLOW)

    def body(carry):
        p, _, acc, run = carry
        copies = page_copies(pt_ref[b, p])
        for cp in copies:
            cp.start()
        for cp in copies:
            cp.wait()
        od, tot = block(_page_rows(kbuf, d), _page_rows(vbuf, d), None, run)
        run = run + tot
        return p - 1, jnp.max(run), acc + od, run

    _, _, acc, _ = lax.while_loop(cond, body,
                                  (jnp.int32(d.n_pages - 1), jnp.max(run0), acc0, run0))
    o_ref[...] = acc.astype(o_ref.dtype)


def sample_sb_attention(page_table, q_s, cache_k, cache_v, layer, zb, k_off, v_off, d):
    R = d.Ts * d.n_heads
    tailb = d.NP // TAIL
    page_shape = (d.page, d.n_kv, d.hd)
    grid_spec = pltpu.PrefetchScalarGridSpec(
        num_scalar_prefetch=1,
        grid=(d.Bd,),
        in_specs=[
            pl.BlockSpec((None, R, d.hd), lambda b, pt: (b, 0, 0)),
            pl.BlockSpec((TAIL, d.KVW), lambda b, pt: (tailb, k_off)),
            pl.BlockSpec((TAIL, d.KVW), lambda b, pt: (tailb, v_off)),
            pl.BlockSpec(memory_space=pl.ANY),
            pl.BlockSpec(memory_space=pl.ANY),
        ],
        out_specs=pl.BlockSpec((None, R, d.hd), lambda b, pt: (b, 0, 0)),
        scratch_shapes=[pltpu.VMEM(page_shape, F32), pltpu.VMEM(page_shape, F32),
                        pltpu.SemaphoreType.DMA((2,))],
    )
    return pl.pallas_call(
        functools.partial(_sample_sb_kernel, d=d, layer=layer),
        grid_spec=grid_spec,
        out_shape=jax.ShapeDtypeStruct((d.Bd, R, d.hd), BF16),
        compiler_params=_cparams(1),
    )(page_table, q_s, zb, zb, cache_k, cache_v)


def _conv_glu_kernel(g_ref, halo_ref, u_ref, w_ref, b_ref, p1_ref, p2_ref, o_ref, *, d):
    m = pl.program_id(1)
    is_tail = m == d.NP // TAIL
    g = g_ref[...]
    gext = jnp.concatenate([halo_ref[...], g], axis=0)
    g1 = pltpu.roll(gext, 1, 0)[SUBLANE:]
    g2 = pltpu.roll(gext, 2, 0)[SUBLANE:]
    r = lax.broadcasted_iota(I32, (TAIL, 1), 0)
    step = jnp.where(r < d.soff, _mod(r, d.n_meta), _mod(r - d.soff, d.Ts))
    g1 = jnp.where(is_tail & (step < 1), p1_ref[...], g1)
    g2 = jnp.where(is_tail & (step < 2), p2_ref[...], g2)
    gc = b_ref[...] + g2 * w_ref[0:1, :] + g1 * w_ref[1:2, :] + g * w_ref[2:3, :]
    o_ref[...] = (gc * jax.nn.sigmoid(gc) * u_ref[...]).astype(o_ref.dtype)


def conv_glu_gate(gu, conv_w, conv_b, layer, prev1, prev2, d):
    assert conv_w.shape[1] == 3
    M = d.M
    tf = _pick_tile(d.d_ff, 5504, LANE)
    nf = d.d_ff // tf
    per_seq = d.nqb
    tail_m = d.NP // TAIL

    def halo_idx(f, m):
        b = m // per_seq
        seq_start = (d.NP + b * d.n_meta + d.n_meta - SUBLANE) // SUBLANE
        normal = (TAIL // SUBLANE) * m - 1
        idx = jnp.where(m % per_seq == 0, seq_start, normal)
        return (jnp.where(m == tail_m, 0, idx), f)

    return pl.pallas_call(
        functools.partial(_conv_glu_kernel, d=d),
        grid=(nf, M // TAIL),
        in_specs=[
            pl.BlockSpec((TAIL, tf), lambda f, m: (m, f)),
            pl.BlockSpec((SUBLANE, tf), halo_idx),
            pl.BlockSpec((TAIL, tf), lambda f, m: (m, nf + f)),
            pl.BlockSpec((None, 3, tf), lambda f, m: (layer, 0, f)),
            pl.BlockSpec((None, 1, tf), lambda f, m: (layer, 0, f)),
            pl.BlockSpec((TAIL, tf), lambda f, m: (0, f)),
            pl.BlockSpec((TAIL, tf), lambda f, m: (0, f)),
        ],
        out_specs=pl.BlockSpec((TAIL, tf), lambda f, m: (m, f)),
        out_shape=jax.ShapeDtypeStruct((M, d.d_ff), BF16),
        compiler_params=_cparams(2),
    )(gu, gu, gu, conv_w, conv_b[:, None, :], prev1, prev2)


def _rope_tables(pos, hd):
    rot = hd // 4
    half = rot // 2
    inv = ROPE_THETA ** (-jnp.arange(half, dtype=F32) / half)
    ang = pos.astype(F32)[:, None] * inv[None, :]
    cos, sin = jnp.cos(ang), jnp.sin(ang)
    n = pos.shape[0]
    ones = jnp.ones((n, hd - rot), F32)
    zeros = jnp.zeros((n, hd - rot), F32)
    zh = jnp.zeros((n, half), F32)
    c = jnp.concatenate([cos, cos, ones], axis=1)
    s1 = jnp.concatenate([zh, sin, zeros], axis=1)
    s2 = jnp.concatenate([-sin, zh, zeros], axis=1)
    return c, s1, s2


def _to_seq(rows, d):
    W = rows.shape[1]
    meta = rows[d.NP: d.NP + d.soff].reshape(d.B, d.n_meta, W)
    prm = rows[: d.NP].reshape(d.B, d.SEQ, W)
    smp = rows[d.NP + d.soff: d.NP + d.soff + d.Bd * d.Ts].reshape(d.Bd, d.Ts, W)
    return jnp.concatenate([meta, prm], axis=1), smp


def _sample_rows(x, d, per_row):
    w = x.shape[1] // per_row
    lo = d.NP + d.soff
    return x[lo: lo + d.Bd * d.Ts].reshape(d.Bd, d.Ts * per_row, w)


def _with_sample_rows(o, o_s, d):
    return lax.dynamic_update_slice(o, o_s.reshape(d.Bd * d.Ts, d.QW), (d.NP + d.soff, 0))


def kernel(x_prompt, x_sample, cache_k_a, cache_v_a, cache_kidx_a, cache_k_b, cache_v_b,
           state_ffn_conv, page_table, meta_tokens, norm_mix, norm_ffn, w_in_a, q_norm_a,
           k_norm_a, w_out_a, w_in_b, w_out_b, w_gate_up, conv_w, conv_b, w_down):
    B, SEQ, D = x_prompt.shape
    Bd, Ts, _ = x_sample.shape
    n_meta = meta_tokens.shape[0]
    _, n_pool, page, n_kv, hd = cache_k_a.shape
    idim = cache_kidx_a.shape[-1]
    QW = w_out_a.shape[1]
    nih = (w_in_a.shape[2] - QW - 2 * n_kv * hd - idim) // idim
    d = Dims(B=B, SEQ=SEQ, D=D, Bd=Bd, Ts=Ts, n_meta=n_meta, page=page,
             n_pages=page_table.shape[1], n_kv=n_kv, hd=hd, n_heads=QW // hd, nih=nih,
             idim=idim, d_ff=w_down.shape[1], depth=norm_mix.shape[0])
    assert SEQ % TAIL == 0 and page == LANE and hd == LANE and idim == LANE
    assert d.soff + Bd * Ts <= TAIL and n_meta % SUBLANE == 0 and 2 <= Ts <= SUBLANE
    assert w_in_a.shape[2] == d.QW + 2 * d.KVW + d.IQW + idim + nih
    assert (Ts * d.nih) % SUBLANE == 0 and d.nih % SUBLANE == 0
    npad = TAIL - d.soff - Bd * Ts

    h = jnp.concatenate([x_prompt.reshape(d.NP, D), jnp.tile(meta_tokens, (B, 1)),
                         x_sample.reshape(Bd * Ts, D), jnp.zeros((npad, D), F32)], axis=0)
    pos = jnp.concatenate([jnp.tile(n_meta + jnp.arange(SEQ), B), jnp.tile(jnp.arange(n_meta), B),
                           jnp.tile(d.past + jnp.arange(Ts), Bd), jnp.zeros((npad,), I32)])
    tabs = _rope_tables(pos, hd)

    outs = {k: [] for k in ("ka_p", "va_p", "ki_p", "kb_p", "vb_p", "cv_p",
                            "ka_s", "va_s", "ki_s", "kb_s", "vb_s", "cv_s")}
    for layer in range(d.depth):
        j = layer // 2
        n = rmsnorm_bf16(h, norm_mix, layer)
        if layer % 2 == 0:
            z = matmul(n, w_in_a, j)
            q, kf, kb, vb, qi, kif, kib, wi, kt, kit = dsa_post(z, q_norm_a, k_norm_a, j, tabs, d)
            o = dsa_prompt_attention(q, kt, vb, qi, kit, wi, d)
            qi_rows = jnp.transpose(qi[:, d.NP + d.soff: d.NP + d.soff + Bd * Ts], (1, 0, 2))
            qi_s = qi_rows.reshape(Bd, Ts * d.nih, idim)
            wi_s = _sample_rows(wi, d, d.nih).reshape(Bd, Ts * d.nih, 1)
            scores = sample_idx_scores(page_table, qi_s, wi_s, cache_kidx_a, j, kib, d)
            sel = sample_select(scores, d)
            o_s = sample_dsa_attention(page_table, _sample_rows(q, d, d.n_heads), sel,
                                       cache_k_a, cache_v_a, j, kb, vb, d)
            o = _with_sample_rows(o, o_s, d)
            vf = z[:, d.QW + d.KVW: d.QW + 2 * d.KVW]
            for name, rows in (("ka", kf), ("va", vf), ("ki", kif)):
                p_, s_ = _to_seq(rows, d)
                outs[name + "_p"].append(p_)
                outs[name + "_s"].append(s_)
            w_out = w_out_a
        else:
            z, zb = matmul(n, w_in_b, j, emit_bf16=True)
            o = sb_prompt_attention(zb, d.n_heads, d.n_heads + d.n_kv, d)
            lo = d.NP + d.soff
            q_s = zb[lo: lo + Bd * Ts, :d.QW].reshape(Bd, Ts * d.n_heads, hd)
            o_s = sample_sb_attention(page_table, q_s, cache_k_b, cache_v_b, j, zb,
                                      d.QW // d.KVW, d.QW // d.KVW + 1, d)
            o = _with_sample_rows(o, o_s, d)
            for name, lo in (("kb", d.QW), ("vb", d.QW + d.KVW)):
                p_, s_ = _to_seq(z[:, lo: lo + d.KVW], d)
                outs[name + "_p"].append(p_)
                outs[name + "_s"].append(s_)
            w_out = w_out_b
        h = matmul(o, w_out, j, resid=h)

        n = rmsnorm_bf16(h, norm_ffn, layer)
        gu = matmul(n, w_gate_up, layer)
        st = state_ffn_conv[layer]
        zrow = jnp.zeros((Bd, 1, d.d_ff), F32)
        pad_s = jnp.zeros((Bd, Ts - 2, d.d_ff), F32)
        p1 = jnp.concatenate([st[:, 1:2], zrow, pad_s], axis=1).reshape(Bd * Ts, d.d_ff)
        p2 = jnp.concatenate([st[:, 0:1], st[:, 1:2], pad_s], axis=1).reshape(Bd * Ts, d.d_ff)
        top = jnp.zeros((d.soff, d.d_ff), F32)
        bot = jnp.zeros((npad, d.d_ff), F32)
        prev1 = jnp.concatenate([top, p1, bot], axis=0)
        prev2 = jnp.concatenate([top, p2, bot], axis=0)
        act = conv_glu_gate(gu, conv_w, conv_b, layer, prev1, prev2, d)
        h = matmul(act, w_down, layer, k_block=0, k_blocks=2, resid=h)
        h = matmul(act, w_down, layer, k_block=1, k_blocks=2, resid=h)
        lo = d.NP + d.soff
        outs["cv_p"].append(jnp.stack(
            [gu[(b + 1) * SEQ - 2: (b + 1) * SEQ, :d.d_ff] for b in range(B)]))
        outs["cv_s"].append(gu[lo: lo + Bd * Ts, :d.d_ff].reshape(Bd, Ts, d.d_ff)[:, -2:])

    def heads(xs):
        a = jnp.stack(xs)
        return a.reshape(a.shape[:3] + (d.n_kv, d.hd))

    y_p = h[: d.NP].reshape(B, SEQ, D)
    y_s = h[d.NP + d.soff: d.NP + d.soff + Bd * Ts].reshape(Bd, Ts, D)
    return (y_p, y_s,
            heads(outs["ka_p"]), heads(outs["va_p"]), jnp.stack(outs["ki_p"]),
            heads(outs["kb_p"]), heads(outs["vb_p"]), jnp.stack(outs["cv_p"]),
            heads(outs["ka_s"]), heads(outs["va_s"]), jnp.stack(outs["ki_s"]),
            heads(outs["kb_s"]), heads(outs["vb_s"]), jnp.stack(outs["cv_s"]))
```

```python
import functools
from typing import NamedTuple

import jax
import jax.numpy as jnp
from jax import lax
from jax.experimental import pallas as pl
from jax.experimental.pallas import tpu as pltpu

F32 = jnp.float32
BF16 = jnp.bfloat16
I32 = jnp.int32

TOPK_MAX = 256
ROPE_THETA = 500000.0
EPS = 1e-6
LANE = 128
SUBLANE = 8
TAIL = 128
NEG = -1e30
INT_MIN = -2 ** 31
VMEM_LIMIT = 56 * 1024 * 1024


class Dims(NamedTuple):
    B: int
    SEQ: int
    D: int
    Bd: int
    Ts: int
    n_meta: int
    page: int
    n_pages: int
    n_kv: int
    hd: int
    n_heads: int
    nih: int
    idim: int
    d_ff: int
    depth: int

    @property
    def NP(self):
        return self.B * self.SEQ

    @property
    def M(self):
        return self.NP + TAIL

    @property
    def QW(self):
        return self.n_heads * self.hd

    @property
    def KVW(self):
        return self.n_kv * self.hd

    @property
    def IQW(self):
        return self.nih * self.idim

    @property
    def G(self):
        return self.n_heads // self.n_kv

    @property
    def soff(self):
        return self.B * self.n_meta

    @property
    def past(self):
        return self.n_pages * self.page

    @property
    def nqb(self):
        return self.SEQ // TAIL

    @property
    def scale(self):
        return self.hd ** -0.5


def _cparams(n_axes):
    return pltpu.CompilerParams(dimension_semantics=("arbitrary",) * n_axes,
                                vmem_limit_bytes=VMEM_LIMIT)


def _pick_tile(total, target, align):
    best = None
    t = align
    while t <= min(total, target):
        if total % t == 0:
            best = t
        t += align
    assert best is not None, (total, target, align)
    return best


def _div(x, n):
    if n & (n - 1) == 0:
        return x >> (n.bit_length() - 1)
    return x // n


def _mod(x, n):
    if n & (n - 1) == 0:
        return x & (n - 1)
    return x % n


def _dot_nt(a, b):
    return lax.dot_general(a, b, (((1,), (1,)), ((), ())), preferred_element_type=F32)


def _dot(a, b):
    return jnp.dot(a, b, preferred_element_type=F32)


def _rmsnorm_kernel(x_ref, g_ref, o_ref):
    x = x_ref[...]
    ms = jnp.mean(x * x, axis=-1, keepdims=True)
    o_ref[...] = (x * lax.rsqrt(ms + EPS) * g_ref[...]).astype(o_ref.dtype)


def rmsnorm_bf16(h, gains, layer):
    M, D = h.shape
    tr = _pick_tile(M, 256, 16)
    return pl.pallas_call(
        _rmsnorm_kernel,
        grid=(M // tr,),
        in_specs=[pl.BlockSpec((tr, D), lambda m: (m, 0)),
                  pl.BlockSpec((None, 1, D), lambda m: (layer, 0, 0))],
        out_specs=pl.BlockSpec((tr, D), lambda m: (m, 0)),
        out_shape=jax.ShapeDtypeStruct((M, D), BF16),
        compiler_params=_cparams(1),
    )(h, gains[:, None, :])


def _matmul_kernel(*refs, has_resid, emit_bf16):
    x_ref, w_ref = refs[0], refs[1]
    pos = 2
    r_ref = None
    if has_resid:
        r_ref = refs[pos]
        pos += 1
    o_ref = refs[pos]
    pos += 1
    ob_ref = None
    if emit_bf16:
        ob_ref = refs[pos]
        pos += 1
    wb_ref = refs[pos]

    @pl.when(pl.program_id(1) == 0)
    def _():
        wb_ref[...] = w_ref[...].astype(BF16)

    acc = _dot(x_ref[...], wb_ref[...])
    if has_resid:
        acc = acc + r_ref[...]
    o_ref[...] = acc
    if emit_bf16:
        ob_ref[...] = acc.astype(BF16)


def matmul(x, w, layer, *, k_block=0, k_blocks=1, resid=None, emit_bf16=False, n_cols=None):
    M, Kx = x.shape
    _, Kw, N = w.shape
    N = N if n_cols is None else n_cols
    assert Kx == Kw and Kx % k_blocks == 0
    K = Kx // k_blocks
    assert K % LANE == 0
    tm = _pick_tile(M, 1040 if K <= 4096 else 640, 16)
    tn_target = 512
    tn = tn_target
    for cand in (tn_target, 256, 128):
        if cand <= tn_target and N % cand == 0:
            tn = cand
            break
    grid = (pl.cdiv(N, tn), M // tm)
    in_specs = [pl.BlockSpec((tm, K), lambda n, m: (m, k_block)),
                pl.BlockSpec((None, K, tn), lambda n, m: (layer, k_block, n))]
    args = [x, w]
    if resid is not None:
        in_specs.append(pl.BlockSpec((tm, tn), lambda n, m: (m, n)))
        args.append(resid)
    out_specs = [pl.BlockSpec((tm, tn), lambda n, m: (m, n))]
    out_shape = [jax.ShapeDtypeStruct((M, N), F32)]
    if emit_bf16:
        out_specs.append(pl.BlockSpec((tm, tn), lambda n, m: (m, n)))
        out_shape.append(jax.ShapeDtypeStruct((M, N), BF16))
    res = pl.pallas_call(
        functools.partial(_matmul_kernel, has_resid=resid is not None, emit_bf16=emit_bf16),
        grid=grid,
        in_specs=in_specs,
        out_specs=out_specs,
        out_shape=out_shape,
        scratch_shapes=[pltpu.VMEM((K, tn), BF16)],
        compiler_params=_cparams(2),
    )(*args)
    return res if emit_bf16 else res[0]


def _rope(x, c, s1, s2, rot):
    half = rot // 2
    return x * c + pltpu.roll(x, half, 1) * s1 + pltpu.roll(x, LANE - half, 1) * s2


def _dsa_post_kernel(z_ref, qg_ref, kg_ref, c_ref, s1_ref, s2_ref,
                     q_ref, kf_ref, kb_ref, vb_ref, qi_ref, kif_ref, kib_ref, wi_ref,
                     kt_ref, kit_ref, *, d):
    c, s1, s2 = c_ref[...], s1_ref[...], s2_ref[...]
    rot = d.hd // 4
    hd = d.hd

    def headnorm(x, g):
        ms = jnp.mean(x * x, axis=-1, keepdims=True)
        return x * lax.rsqrt(ms + EPS) * g

    qg, kg = qg_ref[...], kg_ref[...]
    for h in range(d.n_heads):
        x = z_ref[:, h * hd:(h + 1) * hd]
        q_ref[:, h * hd:(h + 1) * hd] = _rope(headnorm(x, qg), c, s1, s2, rot).astype(BF16)
    off = d.QW
    for h in range(d.n_kv):
        x = z_ref[:, off + h * hd: off + (h + 1) * hd]
        y = _rope(headnorm(x, kg), c, s1, s2, rot)
        kf_ref[:, h * hd:(h + 1) * hd] = y
        kb_ref[:, h * hd:(h + 1) * hd] = y.astype(BF16)
        kt_ref[h * hd:(h + 1) * hd, :] = y.T.astype(BF16)
    off = d.QW + d.KVW
    vb_ref[...] = z_ref[:, off: off + d.KVW].astype(BF16)
    off = d.QW + 2 * d.KVW
    for h in range(d.nih):
        x = z_ref[:, off + h * d.idim: off + (h + 1) * d.idim]
        qi_ref[h] = _rope(x, c, s1, s2, rot).astype(BF16)
    off = d.QW + 2 * d.KVW + d.IQW
    y = _rope(z_ref[:, off: off + d.idim], c, s1, s2, rot)
    kif_ref[...] = y
    kib_ref[...] = y.astype(BF16)
    kit_ref[...] = y.T.astype(BF16)
    off = off + d.idim
    wi_ref[...] = z_ref[:, off: off + d.nih] * (d.nih ** -0.5) * (d.idim ** -0.5)


def dsa_post(z, q_gain, k_gain, layer, tabs, d):
    M = d.M
    tr = TAIL
    c, s1, s2 = tabs
    row = lambda w: pl.BlockSpec((tr, w), lambda m: (m, 0))
    gain = pl.BlockSpec((None, 1, d.hd), lambda m: (layer, 0, 0))
    return pl.pallas_call(
        functools.partial(_dsa_post_kernel, d=d),
        grid=(M // tr,),
        in_specs=[row(z.shape[1]), gain, gain, row(LANE), row(LANE), row(LANE)],
        out_specs=[row(d.QW), row(d.KVW), row(d.KVW), row(d.KVW),
                   pl.BlockSpec((d.nih, tr, d.idim), lambda m: (0, m, 0)),
                   row(d.idim), row(d.idim), row(d.nih),
                   pl.BlockSpec((d.KVW, tr), lambda m: (0, m)),
                   pl.BlockSpec((d.idim, tr), lambda m: (0, m))],
        out_shape=[jax.ShapeDtypeStruct((M, d.QW), BF16),
                   jax.ShapeDtypeStruct((M, d.KVW), F32),
                   jax.ShapeDtypeStruct((M, d.KVW), BF16),
                   jax.ShapeDtypeStruct((M, d.KVW), BF16),
                   jax.ShapeDtypeStruct((d.nih, M, d.idim), BF16),
                   jax.ShapeDtypeStruct((M, d.idim), F32),
                   jax.ShapeDtypeStruct((M, d.idim), BF16),
                   jax.ShapeDtypeStruct((M, d.nih), F32),
                   jax.ShapeDtypeStruct((d.KVW, M), BF16),
                   jax.ShapeDtypeStruct((d.idim, M), BF16)],
        compiler_params=_cparams(1),
    )(z, q_gain[:, None, :], k_gain[:, None, :], c, s1, s2)


def _float_key(x):
    bits = lax.bitcast_convert_type(x, I32)
    return bits ^ ((bits >> 31) & jnp.int32(0x7FFFFFFF))


def _kth_largest_key(key, k):
    kf = jnp.float32(k)

    def count_ge(cand):
        return jnp.sum(jnp.where(key >= cand, 1.0, 0.0), axis=1, keepdims=True)

    t0 = jnp.where(count_ge(jnp.int32(0)) >= kf, jnp.int32(0), jnp.int32(INT_MIN))

    def body(it, t):
        cand = t | lax.shift_left(jnp.int32(1), jnp.int32(30) - it)
        return jnp.where(count_ge(cand) >= kf, cand, t)

    return lax.fori_loop(0, 31, body, t0)


def _stack_heads(x, n, w):
    return jnp.concatenate([x[:, j * w:(j + 1) * w] for j in range(n)], axis=0)


def _prompt_query_meta(s, d):
    is_tail = s >= d.B * d.nqb
    b = jnp.minimum(s // d.nqb, d.B - 1)
    i = s % d.nqb
    r = lax.broadcasted_iota(I32, (TAIL, 1), 0)
    q_seq = jnp.where(is_tail, jnp.where(r < d.soff, _div(r, d.n_meta), -1), b)
    q_pos = jnp.where(is_tail, _mod(r, d.n_meta), d.n_meta + i * TAIL + r)
    return is_tail, b, i, q_seq, q_pos


def _tail_key_meta(d):
    c = lax.broadcasted_iota(I32, (1, TAIL), 1)
    k_seq = jnp.where(c < d.soff, _div(c, d.n_meta), -2)
    k_pos = _mod(c, d.n_meta)
    return k_seq, k_pos


DSA_EXTENT_STEP = 4
IDX_KEY_CHUNK = 256
DSA_GROUPS_PER_STEP = 2


def _dsa_prompt_kernel(qi_ref, wi_ref, kip_ref, kit_ref, q_ref, kp_ref, kt_ref, vp_ref, vt_ref,
                       o_ref, score_ref, sel_ref, wb_ref, *, d, n_sel, gp):
    s = pl.program_id(0)
    g = pl.program_id(1)
    _, b, i, q_seq, q_pos = _prompt_query_meta(s, d)

    def body(npk):
        W = npk + TAIL

        @pl.when(g == 0)
        def _():
            cp = lax.broadcasted_iota(I32, (1, npk), 1)
            valid_p = (q_seq == b) & (d.n_meta + cp <= q_pos)
            tk_seq, tk_pos = _tail_key_meta(d)
            valid_t = (tk_seq == q_seq) & (tk_pos <= q_pos)
            valid = jnp.concatenate([valid_p, valid_t], axis=1)

            wi = wi_ref[...]
            for h in range(d.nih):
                wb_ref[h * TAIL:(h + 1) * TAIL, :] = jnp.broadcast_to(wi[:, h:h + 1], (TAIL, LANE))
            qall = qi_ref[...].reshape(d.nih * TAIL, d.idim)

            def chunk_scores(kchunk_t):
                n = kchunk_t.shape[1]
                w = jnp.maximum(_dot(qall, kchunk_t), 0.0)
                w = w * jnp.concatenate([wb_ref[...]] * (n // LANE), axis=1)
                return jnp.sum(w.reshape(d.nih, TAIL, n), axis=0)

            for c0 in range(0, npk, IDX_KEY_CHUNK):
                c1 = min(npk, c0 + IDX_KEY_CHUNK)
                score_ref[:, c0:c1] = chunk_scores(kip_ref[:, c0:c1])
            score_ref[:, npk:W] = chunk_scores(kit_ref[...])
            key = jnp.where(valid, _float_key(score_ref[:, :W]), jnp.int32(INT_MIN))
            t = _kth_largest_key(key, n_sel)
            sel_ref[:, :W] = jnp.where(valid & (key >= t), 1.0, 0.0)

        G = d.G
        RG = G * TAIL
        hd = d.hd
        qg = _stack_heads(q_ref[...], G * gp, hd)
        sc = jnp.concatenate([
            jnp.concatenate([_dot(qg[c * RG:(c + 1) * RG], kp_ref[c * hd:(c + 1) * hd, :npk]),
                             _dot(qg[c * RG:(c + 1) * RG], kt_ref[c * hd:(c + 1) * hd, :])], axis=1)
            for c in range(gp)], axis=0) * d.scale
        sel = jnp.concatenate([sel_ref[:, :W]] * (G * gp), axis=0) > 0.5
        sc = jnp.where(sel, sc, NEG)
        m = jnp.max(sc, axis=1, keepdims=True)
        p = jnp.exp(sc - m)
        l = jnp.sum(p, axis=1, keepdims=True)
        pb = p.astype(BF16)
        o = jnp.concatenate([
            _dot(pb[c * RG:(c + 1) * RG, :npk], vp_ref[:npk, c * hd:(c + 1) * hd])
            + _dot(pb[c * RG:(c + 1) * RG, npk:], vt_ref[:, c * hd:(c + 1) * hd])
            for c in range(gp)], axis=0) / l
        for j in range(G * gp):
            o_ref[:, j * hd:(j + 1) * hd] = o[j * TAIL:(j + 1) * TAIL].astype(o_ref.dtype)

    extents = sorted({min(d.nqb, e) for e in range(DSA_EXTENT_STEP, d.nqb + DSA_EXTENT_STEP,
                                                    DSA_EXTENT_STEP)})
    for lo, ext in zip([0] + extents[:-1], extents):
        pl.when((i >= lo) & (i < ext))(functools.partial(body, ext * TAIL))


def dsa_prompt_attention(q, kt, vb, qi, kit, wi, d):
    n_sel = min(TOPK_MAX, (d.SEQ + d.n_meta) // 4)
    gp = DSA_GROUPS_PER_STEP if d.n_kv % DSA_GROUPS_PER_STEP == 0 else 1
    nblk = d.B * d.nqb + 1
    tailb = d.NP // TAIL
    gw = gp * d.G * d.hd
    kw = gp * d.hd
    seq_of = lambda s: jnp.minimum(s // d.nqb, d.B - 1)
    in_specs = [
        pl.BlockSpec((d.nih, TAIL, d.idim), lambda s, g: (0, s, 0)),
        pl.BlockSpec((TAIL, d.nih), lambda s, g: (s, 0)),
        pl.BlockSpec((d.idim, d.SEQ), lambda s, g: (0, seq_of(s))),
        pl.BlockSpec((d.idim, TAIL), lambda s, g: (0, tailb)),
        pl.BlockSpec((TAIL, gw), lambda s, g: (s, g)),
        pl.BlockSpec((kw, d.SEQ), lambda s, g: (g, seq_of(s))),
        pl.BlockSpec((kw, TAIL), lambda s, g: (g, tailb)),
        pl.BlockSpec((d.SEQ, kw), lambda s, g: (seq_of(s), g)),
        pl.BlockSpec((TAIL, kw), lambda s, g: (tailb, g)),
    ]
    return pl.pallas_call(
        functools.partial(_dsa_prompt_kernel, d=d, n_sel=n_sel, gp=gp),
        grid=(nblk, d.n_kv // gp),
        in_specs=in_specs,
        out_specs=pl.BlockSpec((TAIL, gw), lambda s, g: (s, g)),
        out_shape=jax.ShapeDtypeStruct((d.M, d.QW), BF16),
        scratch_shapes=[pltpu.VMEM((TAIL, d.SEQ + TAIL), F32),
                        pltpu.VMEM((TAIL, d.SEQ + TAIL), F32),
                        pltpu.VMEM((d.nih * TAIL, LANE), F32)],
        compiler_params=_cparams(2),
    )(qi, wi, kit, kit, q, kt, kt, vb, vb)


SB_UNDERFLOW = 104.0


def _log_sigmoid(z):
    return jnp.minimum(z, 0.0) - jnp.log(1.0 + jnp.exp(-jnp.abs(z)))


def _sb_weights(z, valid, carry):
    ls = _log_sigmoid(z)
    lr = ls - z
    if valid is not None:
        lr = jnp.where(valid, lr, 0.0)
    ci = lax.broadcasted_iota(I32, (LANE, LANE), 0)
    cj = lax.broadcasted_iota(I32, (LANE, LANE), 1)
    upper = jnp.where(ci > cj, 1.0, 0.0).astype(BF16)
    hi = lr.astype(BF16)
    lo = (lr - hi.astype(F32)).astype(BF16)
    sfx = _dot(hi, upper) + _dot(lo, upper)
    a = jnp.exp(ls + sfx + carry)
    if valid is not None:
        a = jnp.where(valid, a, 0.0)
    return a.astype(BF16), jnp.sum(lr, axis=1, keepdims=True)


SB_GROUPS_PER_STEP = 8


def _sb_prompt_kernel(q_ref, kp_ref, kt_ref, vp_ref, vt_ref, o_ref, acc_ref, run_ref, *, d, gp):
    s = pl.program_id(0)
    G = d.G
    RG = G * TAIL
    _, _, i, q_seq, q_pos = _prompt_query_meta(s, d)
    q_seq = jnp.concatenate([q_seq] * (G * gp), axis=0)
    q_pos = jnp.concatenate([q_pos] * (G * gp), axis=0)
    qg = _stack_heads(q_ref[...], G * gp, d.hd)
    cl = lax.broadcasted_iota(I32, (1, TAIL), 1)

    def weighted(kblk, vblk, valid, run):
        z = jnp.concatenate(
            [_dot_nt(qg[c * RG:(c + 1) * RG], kblk[:, c * d.hd:(c + 1) * d.hd]) for c in range(gp)],
            axis=0) * d.scale
        a, tot = _sb_weights(z, valid, run)
        o = jnp.concatenate(
            [_dot(a[c * RG:(c + 1) * RG], vblk[:, c * d.hd:(c + 1) * d.hd]) for c in range(gp)],
            axis=0)
        return o, tot

    def block(kb, valid, run):
        start = pl.multiple_of(kb * TAIL, TAIL)
        return weighted(kp_ref[pl.ds(start, TAIL), :], vp_ref[pl.ds(start, TAIL), :], valid, run)

    o, tot = block(i, (d.n_meta + i * TAIL + cl) < q_pos, 0.0)
    acc_ref[...] = o
    run_ref[...] = tot

    def cond(c):
        kb, top = c
        return (kb >= 0) & (top > -SB_UNDERFLOW)

    def body(c):
        kb, _ = c
        o, tot = block(kb, None, run_ref[...])
        acc_ref[...] += o
        run = run_ref[...] + tot
        run_ref[...] = run
        return kb - 1, jnp.max(run)

    _, top = lax.while_loop(cond, body, (i - 1, jnp.max(tot)))

    @pl.when(top > -SB_UNDERFLOW)
    def _():
        tk_seq, tk_pos = _tail_key_meta(d)
        valid = (tk_seq == q_seq) & (tk_pos < q_pos)
        o, _ = weighted(kt_ref[...], vt_ref[...], valid, run_ref[...])
        acc_ref[...] += o

    for j in range(G * gp):
        o_ref[:, j * d.hd:(j + 1) * d.hd] = acc_ref[j * TAIL:(j + 1) * TAIL, :].astype(o_ref.dtype)


def sb_prompt_attention(zb, k_off, v_off, d):
    gp = SB_GROUPS_PER_STEP if d.n_kv % SB_GROUPS_PER_STEP == 0 else 1
    nblk = d.B * d.nqb + 1
    tailb = d.NP // TAIL
    gw = gp * d.G * d.hd
    kw = gp * d.hd
    assert k_off % kw == 0 and v_off % kw == 0
    kblk, vblk = k_off // kw, v_off // kw
    seq_of = lambda s: jnp.minimum(s // d.nqb, d.B - 1)
    in_specs = [
        pl.BlockSpec((TAIL, gw), lambda s, g: (s, g)),
        pl.BlockSpec((d.SEQ, kw), lambda s, g: (seq_of(s), kblk + g)),
        pl.BlockSpec((TAIL, kw), lambda s, g: (tailb, kblk + g)),
        pl.BlockSpec((d.SEQ, kw), lambda s, g: (seq_of(s), vblk + g)),
        pl.BlockSpec((TAIL, kw), lambda s, g: (tailb, vblk + g)),
    ]
    rows = gp * d.G * TAIL
    return pl.pallas_call(
        functools.partial(_sb_prompt_kernel, d=d, gp=gp),
        grid=(nblk, d.n_kv // gp),
        in_specs=in_specs,
        out_specs=pl.BlockSpec((TAIL, gw), lambda s, g: (s, g)),
        out_shape=jax.ShapeDtypeStruct((d.M, d.QW), BF16),
        scratch_shapes=[pltpu.VMEM((rows, d.hd), F32), pltpu.VMEM((rows, 1), F32)],
        compiler_params=_cparams(2),
    )(zb, zb, zb, zb, zb)


def _block_diag_q(q, d):
    R = q.shape[0]
    rho = lax.broadcasted_iota(I32, (R, 1), 0)
    grp = _div(_mod(rho, d.n_heads), d.G)
    zero = jnp.zeros_like(q)
    return jnp.concatenate([jnp.where(grp == g, q, zero) for g in range(d.n_kv)], axis=1)


def _own_group_slab(oall, d):
    R = oall.shape[0]
    rho = lax.broadcasted_iota(I32, (R, 1), 0)
    grp = _div(_mod(rho, d.n_heads), d.G)
    out = jnp.zeros((R, d.hd), F32)
    for g in range(d.n_kv):
        out = out + jnp.where(grp == g, oall[:, g * d.hd:(g + 1) * d.hd], 0.0)
    return out


def _page_rows(page_ref, d):
    return jnp.concatenate([page_ref[:, h, :] for h in range(d.n_kv)], axis=1).astype(BF16)


IDX_PAGES_PER_STEP = 8
ATTN_PAGES_PER_STEP = 8


def _pages_per_step(n_pages, target):
    return max(g for g in range(1, target + 1) if n_pages % g == 0)


def _page_index_map(layer, j, pg, npg, trailing):
    def index_map(b, p, pt):
        return (layer, pt[b, jnp.minimum(p * pg + j, npg - 1)]) + (0,) * trailing
    return index_map


def _sample_idx_kernel(pt_ref, qi_ref, wi_ref, *refs, d, pg):
    kc_refs, kt_ref, o_ref = refs[:pg], refs[pg], refs[pg + 1]
    p = pl.program_id(1)
    last = d.n_pages // pg

    def scores(kblk):
        w = jnp.maximum(_dot_nt(qi_ref[...], kblk), 0.0) * wi_ref[...]
        rows = [jnp.sum(w[t * d.nih:(t + 1) * d.nih], axis=0, keepdims=True) for t in range(d.Ts)]
        rows.append(jnp.zeros((SUBLANE - d.Ts, kblk.shape[0]), F32))
        return jnp.concatenate(rows, axis=0)

    @pl.when(p < last)
    def _():
        o_ref[...] = scores(jnp.concatenate([r[...].astype(BF16) for r in kc_refs], axis=0))

    @pl.when(p == last)
    def _():
        o_ref[...] = jnp.zeros_like(o_ref)
        o_ref[:, :TAIL] = scores(kt_ref[...])


def sample_idx_scores(page_table, qi_s, wi_s, cache_ki, layer, kib, d):
    R = d.Ts * d.nih
    npg = d.n_pages
    pg = _pages_per_step(npg, IDX_PAGES_PER_STEP)
    tailb = d.NP // TAIL
    grid_spec = pltpu.PrefetchScalarGridSpec(
        num_scalar_prefetch=1,
        grid=(d.Bd, npg // pg + 1),
        in_specs=[
            pl.BlockSpec((None, R, d.idim), lambda b, p, pt: (b, 0, 0)),
            pl.BlockSpec((None, R, 1), lambda b, p, pt: (b, 0, 0)),
        ] + [
            pl.BlockSpec((None, None, d.page, d.idim), _page_index_map(layer, j, pg, npg, 2))
            for j in range(pg)
        ] + [
            pl.BlockSpec((TAIL, d.idim), lambda b, p, pt: (tailb, 0)),
        ],
        out_specs=pl.BlockSpec((None, SUBLANE, pg * LANE), lambda b, p, pt: (b, 0, p)),
    )
    return pl.pallas_call(
        functools.partial(_sample_idx_kernel, d=d, pg=pg),
        grid_spec=grid_spec,
        out_shape=jax.ShapeDtypeStruct((d.Bd, SUBLANE, (npg + pg) * LANE), F32),
        compiler_params=_cparams(2),
    )(page_table, qi_s, wi_s, *([cache_ki] * pg), kib)


def _sample_select_kernel(s_ref, o_ref, *, d, n_sel):
    b = pl.program_id(0)
    L = s_ref.shape[1]
    col = lax.broadcasted_iota(I32, (1, L), 1)
    t = lax.broadcasted_iota(I32, (SUBLANE, 1), 0)
    snew = col - d.past - d.soff - d.Ts * b
    valid = (col < d.past) | ((snew >= 0) & (snew < d.Ts) & (snew <= t))
    valid = valid & (t < d.Ts)
    key = jnp.where(valid, _float_key(s_ref[...]), jnp.int32(INT_MIN))
    thr = _kth_largest_key(key, n_sel)
    o_ref[...] = jnp.where(valid & (key >= thr), 1.0, 0.0)


def sample_select(scores, d):
    n_sel = min(TOPK_MAX, (d.past + d.Ts) // 4)
    L = scores.shape[2]
    spec = pl.BlockSpec((None, SUBLANE, L), lambda b: (b, 0, 0))
    return pl.pallas_call(
        functools.partial(_sample_select_kernel, d=d, n_sel=n_sel),
        grid=(d.Bd,),
        in_specs=[spec],
        out_specs=spec,
        out_shape=jax.ShapeDtypeStruct(scores.shape, F32),
        compiler_params=_cparams(1),
    )(scores)


def _sample_dsa_attn_kernel(pt_ref, q_ref, sel_ref, *refs, d, pg):
    kc_refs, vc_refs = refs[:pg], refs[pg:2 * pg]
    kt_ref, vt_ref, o_ref, expand_ref, own_ref, m_ref, l_ref, acc_ref = refs[2 * pg:]
    p = pl.program_id(1)
    last = d.n_pages // pg
    R = q_ref.shape[0]
    W = d.page * d.n_kv

    @pl.when(p == 0)
    def _():
        key = lax.broadcasted_iota(I32, (d.page, W), 0)
        col = lax.broadcasted_iota(I32, (d.page, W), 1)
        expand_ref[...] = jnp.where(_div(col, d.n_kv) == key, 1.0, 0.0).astype(BF16)
        rho = lax.broadcasted_iota(I32, (R, W), 0)
        col = lax.broadcasted_iota(I32, (R, W), 1)
        own_ref[...] = jnp.where(_mod(col, d.n_kv) == _div(_mod(rho, d.n_heads), d.G), 1.0, 0.0)
        m_ref[...] = jnp.full_like(m_ref, NEG)
        l_ref[...] = jnp.zeros_like(l_ref)
        acc_ref[...] = jnp.zeros_like(acc_ref)

    def per_row(sel):
        return jnp.concatenate(
            [jnp.broadcast_to(sel[t:t + 1], (d.n_heads, sel.shape[1])) for t in range(d.Ts)], axis=0)

    def update(sc, mask, pv):
        sc = jnp.where(mask, sc, NEG)
        m_old = m_ref[...]
        m_new = jnp.maximum(m_old, jnp.max(sc, axis=1, keepdims=True))
        alpha = jnp.exp(m_old - m_new)
        pr = jnp.where(mask, jnp.exp(sc - m_new), 0.0)
        l_ref[...] = alpha * l_ref[...] + jnp.sum(pr, axis=1, keepdims=True)
        acc_ref[...] = alpha * acc_ref[...] + pv(pr.astype(BF16))
        m_ref[...] = m_new

    @pl.when(p < last)
    def _():
        kflat = jnp.concatenate([r[...].reshape(W, d.hd) for r in kc_refs], axis=0).astype(BF16)
        vflat = jnp.concatenate([r[...].reshape(W, d.hd) for r in vc_refs], axis=0).astype(BF16)
        sc = _dot_nt(q_ref[...], kflat) * d.scale
        sel = sel_ref[...].astype(BF16)
        selx = jnp.concatenate([_dot(sel[:, j * d.page:(j + 1) * d.page], expand_ref[...])
                                for j in range(pg)], axis=1)
        own = jnp.concatenate([own_ref[...]] * pg, axis=1)
        update(sc, per_row(selx) * own > 0.5, lambda pr: _dot(pr, vflat))

    @pl.when(p == last)
    def _():
        sc = _dot_nt(_block_diag_q(q_ref[...], d), kt_ref[...]) * d.scale
        update(sc, per_row(sel_ref[:, :TAIL]) > 0.5,
               lambda pr: _own_group_slab(_dot(pr, vt_ref[...]), d))
        o_ref[...] = (acc_ref[...] / l_ref[...]).astype(o_ref.dtype)


def sample_dsa_attention(page_table, q_s, sel, cache_k, cache_v, layer, kb, vb, d):
    R = d.Ts * d.n_heads
    npg = d.n_pages
    pg = _pages_per_step(npg, ATTN_PAGES_PER_STEP)
    assert sel.shape[2] >= (npg + pg) * LANE
    tailb = d.NP // TAIL
    page_specs = [pl.BlockSpec((None, None, d.page, d.n_kv, d.hd),
                               _page_index_map(layer, j, pg, npg, 3)) for j in range(pg)]
    tail_spec = pl.BlockSpec((TAIL, d.KVW), lambda b, p, pt: (tailb, 0))
    grid_spec = pltpu.PrefetchScalarGridSpec(
        num_scalar_prefetch=1,
        grid=(d.Bd, npg // pg + 1),
        in_specs=[
            pl.BlockSpec((None, R, d.hd), lambda b, p, pt: (b, 0, 0)),
            pl.BlockSpec((None, SUBLANE, pg * LANE), lambda b, p, pt: (b, 0, p)),
        ] + page_specs + page_specs + [tail_spec, tail_spec],
        out_specs=pl.BlockSpec((None, R, d.hd), lambda b, p, pt: (b, 0, 0)),
        scratch_shapes=[pltpu.VMEM((d.page, d.page * d.n_kv), BF16),
                        pltpu.VMEM((R, d.page * d.n_kv), F32), pltpu.VMEM((R, 1), F32),
                        pltpu.VMEM((R, 1), F32), pltpu.VMEM((R, d.hd), F32)],
    )
    return pl.pallas_call(
        functools.partial(_sample_dsa_attn_kernel, d=d, pg=pg),
        grid_spec=grid_spec,
        out_shape=jax.ShapeDtypeStruct((d.Bd, R, d.hd), BF16),
        compiler_params=_cparams(2),
    )(page_table, q_s, sel, *([cache_k] * pg), *([cache_v] * pg), kb, vb)


def _sample_sb_kernel(pt_ref, q_ref, kt_ref, vt_ref, kc_hbm, vc_hbm, o_ref,
                      kbuf, vbuf, sems, *, d, layer):
    b = pl.program_id(0)
    R = q_ref.shape[0]
    qbd = _block_diag_q(q_ref[...], d)

    def block(kblk, vblk, valid, run):
        a, tot = _sb_weights(_dot_nt(qbd, kblk) * d.scale, valid, run)
        return _own_group_slab(_dot(a, vblk), d), tot

    rho = lax.broadcasted_iota(I32, (R, 1), 0)
    tq = _div(rho, d.n_heads)
    c = lax.broadcasted_iota(I32, (1, TAIL), 1)
    snew = c - d.soff - d.Ts * b
    acc0, run0 = block(kt_ref[...], vt_ref[...], (snew >= 0) & (snew < tq), 0.0)

    def page_copies(page):
        return (pltpu.make_async_copy(kc_hbm.at[layer, page], kbuf, sems.at[0]),
                pltpu.make_async_copy(vc_hbm.at[layer, page], vbuf, sems.at[1]))

    def cond(carry):
        p, top, _, _ = carry
        return (p >= 0) & (top > -SB_UNDERFLOW)

    def body(carry):
        p, _, acc, run = carry
        copies = page_copies(pt_ref[b, p])
        for cp in copies:
            cp.start()
        for cp in copies:
            cp.wait()
        od, tot = block(_page_rows(kbuf, d), _page_rows(vbuf, d), None, run)
        run = run + tot
        return p - 1, jnp.max(run), acc + od, run

    _, _, acc, _ = lax.while_loop(cond, body,
                                  (jnp.int32(d.n_pages - 1), jnp.max(run0), acc0, run0))
    o_ref[...] = acc.astype(o_ref.dtype)


def sample_sb_attention(page_table, q_s, cache_k, cache_v, layer, zb, k_off, v_off, d):
    R = d.Ts * d.n_heads
    tailb = d.NP // TAIL
    page_shape = (d.page, d.n_kv, d.hd)
    grid_spec = pltpu.PrefetchScalarGridSpec(
        num_scalar_prefetch=1,
        grid=(d.Bd,),
        in_specs=[
            pl.BlockSpec((None, R, d.hd), lambda b, pt: (b, 0, 0)),
            pl.BlockSpec((TAIL, d.KVW), lambda b, pt: (tailb, k_off)),
            pl.BlockSpec((TAIL, d.KVW), lambda b, pt: (tailb, v_off)),
            pl.BlockSpec(memory_space=pl.ANY),
            pl.BlockSpec(memory_space=pl.ANY),
        ],
        out_specs=pl.BlockSpec((None, R, d.hd), lambda b, pt: (b, 0, 0)),
        scratch_shapes=[pltpu.VMEM(page_shape, F32), pltpu.VMEM(page_shape, F32),
                        pltpu.SemaphoreType.DMA((2,))],
    )
    return pl.pallas_call(
        functools.partial(_sample_sb_kernel, d=d, layer=layer),
        grid_spec=grid_spec,
        out_shape=jax.ShapeDtypeStruct((d.Bd, R, d.hd), BF16),
        compiler_params=_cparams(1),
    )(page_table, q_s, zb, zb, cache_k, cache_v)


def _conv_glu_kernel(g_ref, halo_ref, u_ref, w_ref, b_ref, p1_ref, p2_ref, o_ref, *, d):
    m = pl.program_id(1)
    is_tail = m == d.NP // TAIL
    g = g_ref[...]
    gext = jnp.concatenate([halo_ref[...], g], axis=0)
    g1 = pltpu.roll(gext, 1, 0)[SUBLANE:]
    g2 = pltpu.roll(gext, 2, 0)[SUBLANE:]
    r = lax.broadcasted_iota(I32, (TAIL, 1), 0)
    step = jnp.where(r < d.soff, _mod(r, d.n_meta), _mod(r - d.soff, d.Ts))
    g1 = jnp.where(is_tail & (step < 1), p1_ref[...], g1)
    g2 = jnp.where(is_tail & (step < 2), p2_ref[...], g2)
    gc = b_ref[...] + g2 * w_ref[0:1, :] + g1 * w_ref[1:2, :] + g * w_ref[2:3, :]
    o_ref[...] = (gc * jax.nn.sigmoid(gc) * u_ref[...]).astype(o_ref.dtype)


def conv_glu_gate(gu, conv_w, conv_b, layer, prev1, prev2, d):
    assert conv_w.shape[1] == 3
    M = d.M
    tf = _pick_tile(d.d_ff, 5504, LANE)
    nf = d.d_ff // tf
    per_seq = d.nqb
    tail_m = d.NP // TAIL

    def halo_idx(f, m):
        b = m // per_seq
        seq_start = (d.NP + b * d.n_meta + d.n_meta - SUBLANE) // SUBLANE
        normal = (TAIL // SUBLANE) * m - 1
        idx = jnp.where(m % per_seq == 0, seq_start, normal)
        return (jnp.where(m == tail_m, 0, idx), f)

    return pl.pallas_call(
        functools.partial(_conv_glu_kernel, d=d),
        grid=(nf, M // TAIL),
        in_specs=[
            pl.BlockSpec((TAIL, tf), lambda f, m: (m, f)),
            pl.BlockSpec((SUBLANE, tf), halo_idx),
            pl.BlockSpec((TAIL, tf), lambda f, m: (m, nf + f)),
            pl.BlockSpec((None, 3, tf), lambda f, m: (layer, 0, f)),
            pl.BlockSpec((None, 1, tf), lambda f, m: (layer, 0, f)),
            pl.BlockSpec((TAIL, tf), lambda f, m: (0, f)),
            pl.BlockSpec((TAIL, tf), lambda f, m: (0, f)),
        ],
        out_specs=pl.BlockSpec((TAIL, tf), lambda f, m: (m, f)),
        out_shape=jax.ShapeDtypeStruct((M, d.d_ff), BF16),
        compiler_params=_cparams(2),
    )(gu, gu, gu, conv_w, conv_b[:, None, :], prev1, prev2)


def _rope_tables(pos, hd):
    rot = hd // 4
    half = rot // 2
    inv = ROPE_THETA ** (-jnp.arange(half, dtype=F32) / half)
    ang = pos.astype(F32)[:, None] * inv[None, :]
    cos, sin = jnp.cos(ang), jnp.sin(ang)
    n = pos.shape[0]
    ones = jnp.ones((n, hd - rot), F32)
    zeros = jnp.zeros((n, hd - rot), F32)
    zh = jnp.zeros((n, half), F32)
    c = jnp.concatenate([cos, cos, ones], axis=1)
    s1 = jnp.concatenate([zh, sin, zeros], axis=1)
    s2 = jnp.concatenate([-sin, zh, zeros], axis=1)
    return c, s1, s2


def _to_seq(rows, d):
    W = rows.shape[1]
    meta = rows[d.NP: d.NP + d.soff].reshape(d.B, d.n_meta, W)
    prm = rows[: d.NP].reshape(d.B, d.SEQ, W)
    smp = rows[d.NP + d.soff: d.NP + d.soff + d.Bd * d.Ts].reshape(d.Bd, d.Ts, W)
    return jnp.concatenate([meta, prm], axis=1), smp


def _sample_rows(x, d, per_row):
    w = x.shape[1] // per_row
    lo = d.NP + d.soff
    return x[lo: lo + d.Bd * d.Ts].reshape(d.Bd, d.Ts * per_row, w)


def _with_sample_rows(o, o_s, d):
    return lax.dynamic_update_slice(o, o_s.reshape(d.Bd * d.Ts, d.QW), (d.NP + d.soff, 0))


def kernel(x_prompt, x_sample, cache_k_a, cache_v_a, cache_kidx_a, cache_k_b, cache_v_b,
           state_ffn_conv, page_table, meta_tokens, norm_mix, norm_ffn, w_in_a, q_norm_a,
           k_norm_a, w_out_a, w_in_b, w_out_b, w_gate_up, conv_w, conv_b, w_down):
    B, SEQ, D = x_prompt.shape
    Bd, Ts, _ = x_sample.shape
    n_meta = meta_tokens.shape[0]
    _, n_pool, page, n_kv, hd = cache_k_a.shape
    idim = cache_kidx_a.shape[-1]
    QW = w_out_a.shape[1]
    nih = (w_in_a.shape[2] - QW - 2 * n_kv * hd - idim) // idim
    d = Dims(B=B, SEQ=SEQ, D=D, Bd=Bd, Ts=Ts, n_meta=n_meta, page=page,
             n_pages=page_table.shape[1], n_kv=n_kv, hd=hd, n_heads=QW // hd, nih=nih,
             idim=idim, d_ff=w_down.shape[1], depth=norm_mix.shape[0])
    assert SEQ % TAIL == 0 and page == LANE and hd == LANE and idim == LANE
    assert d.soff + Bd * Ts <= TAIL and n_meta % SUBLANE == 0 and 2 <= Ts <= SUBLANE
    assert w_in_a.shape[2] == d.QW + 2 * d.KVW + d.IQW + idim + nih
    assert (Ts * d.nih) % SUBLANE == 0 and d.nih % SUBLANE == 0
    npad = TAIL - d.soff - Bd * Ts

    h = jnp.concatenate([x_prompt.reshape(d.NP, D), jnp.tile(meta_tokens, (B, 1)),
                         x_sample.reshape(Bd * Ts, D), jnp.zeros((npad, D), F32)], axis=0)
    pos = jnp.concatenate([jnp.tile(n_meta + jnp.arange(SEQ), B), jnp.tile(jnp.arange(n_meta), B),
                           jnp.tile(d.past + jnp.arange(Ts), Bd), jnp.zeros((npad,), I32)])
    tabs = _rope_tables(pos, hd)

    outs = {k: [] for k in ("ka_p", "va_p", "ki_p", "kb_p", "vb_p", "cv_p",
                            "ka_s", "va_s", "ki_s", "kb_s", "vb_s", "cv_s")}
    for layer in range(d.depth):
        j = layer // 2
        n = rmsnorm_bf16(h, norm_mix, layer)
        if layer % 2 == 0:
            z = matmul(n, w_in_a, j)
            q, kf, kb, vb, qi, kif, kib, wi, kt, kit = dsa_post(z, q_norm_a, k_norm_a, j, tabs, d)
            o = dsa_prompt_attention(q, kt, vb, qi, kit, wi, d)
            qi_rows = jnp.transpose(qi[:, d.NP + d.soff: d.NP + d.soff + Bd * Ts], (1, 0, 2))
            qi_s = qi_rows.reshape(Bd, Ts * d.nih, idim)
            wi_s = _sample_rows(wi, d, d.nih).reshape(Bd, Ts * d.nih, 1)
            scores = sample_idx_scores(page_table, qi_s, wi_s, cache_kidx_a, j, kib, d)
            sel = sample_select(scores, d)
            o_s = sample_dsa_attention(page_table, _sample_rows(q, d, d.n_heads), sel,
                                       cache_k_a, cache_v_a, j, kb, vb, d)
            o = _with_sample_rows(o, o_s, d)
            vf = z[:, d.QW + d.KVW: d.QW + 2 * d.KVW]
            for name, rows in (("ka", kf), ("va", vf), ("ki", kif)):
                p_, s_ = _to_seq(rows, d)
                outs[name + "_p"].append(p_)
                outs[name + "_s"].append(s_)
            w_out = w_out_a
        else:
            z, zb = matmul(n, w_in_b, j, emit_bf16=True)
            o = sb_prompt_attention(zb, d.QW, d.QW + d.KVW, d)
            lo = d.NP + d.soff
            q_s = zb[lo: lo + Bd * Ts, :d.QW].reshape(Bd, Ts * d.n_heads, hd)
            o_s = sample_sb_attention(page_table, q_s, cache_k_b, cache_v_b, j, zb,
                                      d.QW // d.KVW, d.QW // d.KVW + 1, d)
            o = _with_sample_rows(o, o_s, d)
            for name, lo in (("kb", d.QW), ("vb", d.QW + d.KVW)):
                p_, s_ = _to_seq(z[:, lo: lo + d.KVW], d)
                outs[name + "_p"].append(p_)
                outs[name + "_s"].append(s_)
            w_out = w_out_b
        h = matmul(o, w_out, j, resid=h)

        n = rmsnorm_bf16(h, norm_ffn, layer)
        gu = matmul(n, w_gate_up, layer)
        st = state_ffn_conv[layer]
        zrow = jnp.zeros((Bd, 1, d.d_ff), F32)
        pad_s = jnp.zeros((Bd, Ts - 2, d.d_ff), F32)
        p1 = jnp.concatenate([st[:, 1:2], zrow, pad_s], axis=1).reshape(Bd * Ts, d.d_ff)
        p2 = jnp.concatenate([st[:, 0:1], st[:, 1:2], pad_s], axis=1).reshape(Bd * Ts, d.d_ff)
        top = jnp.zeros((d.soff, d.d_ff), F32)
        bot = jnp.zeros((npad, d.d_ff), F32)
        prev1 = jnp.concatenate([top, p1, bot], axis=0)
        prev2 = jnp.concatenate([top, p2, bot], axis=0)
        act = conv_glu_gate(gu, conv_w, conv_b, layer, prev1, prev2, d)
        h = matmul(act, w_down, layer, k_block=0, k_blocks=2, resid=h)
        h = matmul(act, w_down, layer, k_block=1, k_blocks=2, resid=h)
        lo = d.NP + d.soff
        outs["cv_p"].append(jnp.stack(
            [gu[(b + 1) * SEQ - 2: (b + 1) * SEQ, :d.d_ff] for b in range(B)]))
        outs["cv_s"].append(gu[lo: lo + Bd * Ts, :d.d_ff].reshape(Bd, Ts, d.d_ff)[:, -2:])

    def heads(xs):
        a = jnp.stack(xs)
        return a.reshape(a.shape[:3] + (d.n_kv, d.hd))

    y_p = h[: d.NP].reshape(B, SEQ, D)
    y_s = h[d.NP + d.soff: d.NP + d.soff + Bd * Ts].reshape(Bd, Ts, D)
    return (y_p, y_s,
            heads(outs["ka_p"]), heads(outs["va_p"]), jnp.stack(outs["ki_p"]),
            heads(outs["kb_p"]), heads(outs["vb_p"]), jnp.stack(outs["cv_p"]),
            heads(outs["ka_s"]), heads(outs["va_s"]), jnp.stack(outs["ki_s"]),
            heads(outs["kb_s"]), heads(outs["vb_s"]), jnp.stack(outs["cv_s"]))
```

```python
import functools
from typing import NamedTuple

import jax
import jax.numpy as jnp
from jax import lax
from jax.experimental import pallas as pl
from jax.experimental.pallas import tpu as pltpu

F32 = jnp.float32
BF16 = jnp.bfloat16
I32 = jnp.int32

TOPK_MAX = 256
ROPE_THETA = 500000.0
EPS = 1e-6
LANE = 128
SUBLANE = 8
TAIL = 128
NEG = -1e30
INT_MIN = -2 ** 31
VMEM_LIMIT = 56 * 1024 * 1024


class Dims(NamedTuple):
    B: int
    SEQ: int
    D: int
    Bd: int
    Ts: int
    n_meta: int
    page: int
    n_pages: int
    n_kv: int
    hd: int
    n_heads: int
    nih: int
    idim: int
    d_ff: int
    depth: int

    @property
    def NP(self):
        return self.B * self.SEQ

    @property
    def M(self):
        return self.NP + TAIL

    @property
    def QW(self):
        return self.n_heads * self.hd

    @property
    def KVW(self):
        return self.n_kv * self.hd

    @property
    def IQW(self):
        return self.nih * self.idim

    @property
    def G(self):
        return self.n_heads // self.n_kv

    @property
    def soff(self):
        return self.B * self.n_meta

    @property
    def past(self):
        return self.n_pages * self.page

    @property
    def nqb(self):
        return self.SEQ // TAIL

    @property
    def scale(self):
        return self.hd ** -0.5


def _cparams(n_axes):
    return pltpu.CompilerParams(dimension_semantics=("arbitrary",) * n_axes,
                                vmem_limit_bytes=VMEM_LIMIT)


def _pick_tile(total, target, align):
    best = None
    t = align
    while t <= min(total, target):
        if total % t == 0:
            best = t
        t += align
    assert best is not None, (total, target, align)
    return best


def _div(x, n):
    if n & (n - 1) == 0:
        return x >> (n.bit_length() - 1)
    return x // n


def _mod(x, n):
    if n & (n - 1) == 0:
        return x & (n - 1)
    return x % n


def _dot_nt(a, b):
    return lax.dot_general(a, b, (((1,), (1,)), ((), ())), preferred_element_type=F32)


def _dot(a, b):
    return jnp.dot(a, b, preferred_element_type=F32)


def _rmsnorm_kernel(x_ref, g_ref, o_ref):
    x = x_ref[...]
    ms = jnp.mean(x * x, axis=-1, keepdims=True)
    o_ref[...] = (x * lax.rsqrt(ms + EPS) * g_ref[...]).astype(o_ref.dtype)


def rmsnorm_bf16(h, gains, layer):
    M, D = h.shape
    tr = _pick_tile(M, 512, 16)
    return pl.pallas_call(
        _rmsnorm_kernel,
        grid=(M // tr,),
        in_specs=[pl.BlockSpec((tr, D), lambda m: (m, 0)),
                  pl.BlockSpec((None, 1, D), lambda m: (layer, 0, 0))],
        out_specs=pl.BlockSpec((tr, D), lambda m: (m, 0)),
        out_shape=jax.ShapeDtypeStruct((M, D), BF16),
        compiler_params=_cparams(1),
    )(h, gains[:, None, :])


def _matmul_kernel(*refs, has_resid, emit_bf16, transposed_w):
    x_ref, w_ref = refs[0], refs[1]
    pos = 2
    r_ref = None
    if has_resid:
        r_ref = refs[pos]
        pos += 1
    o_ref = refs[pos]
    pos += 1
    ob_ref = None
    if emit_bf16:
        ob_ref = refs[pos]
        pos += 1
    wb_ref = refs[pos]

    @pl.when(pl.program_id(1) == 0)
    def _():
        wb_ref[...] = w_ref[...].astype(BF16)

    acc = (_dot_nt if transposed_w else _dot)(x_ref[...], wb_ref[...])
    if has_resid:
        acc = acc + r_ref[...]
    o_ref[...] = acc
    if emit_bf16:
        ob_ref[...] = acc.astype(BF16)


def matmul(x, w, layer, *, k_block=0, k_blocks=1, resid=None, emit_bf16=False, n_cols=None,
           transposed_w=False):
    M, Kx = x.shape
    _, Kw, N = w.shape
    N = N if n_cols is None else n_cols
    assert Kx == Kw and Kx % k_blocks == 0
    K = Kx // k_blocks
    assert K % LANE == 0
    tm = _pick_tile(M, 1040 if K <= 4096 else 640, 16)
    tn_target = 512
    tn = tn_target
    for cand in (tn_target, 256, 128):
        if cand <= tn_target and N % cand == 0:
            tn = cand
            break
    grid = (pl.cdiv(N, tn), M // tm)
    in_specs = [pl.BlockSpec((tm, K), lambda n, m: (m, k_block))]
    if transposed_w:
        in_specs.append(pl.BlockSpec((None, tn, K), lambda n, m: (layer, n, k_block)))
        args = [x, jnp.swapaxes(w, 1, 2)]
    else:
        in_specs.append(pl.BlockSpec((None, K, tn), lambda n, m: (layer, k_block, n)))
        args = [x, w]
    if resid is not None:
        in_specs.append(pl.BlockSpec((tm, tn), lambda n, m: (m, n)))
        args.append(resid)
    out_specs = [pl.BlockSpec((tm, tn), lambda n, m: (m, n))]
    out_shape = [jax.ShapeDtypeStruct((M, N), F32)]
    if emit_bf16:
        out_specs.append(pl.BlockSpec((tm, tn), lambda n, m: (m, n)))
        out_shape.append(jax.ShapeDtypeStruct((M, N), BF16))
    res = pl.pallas_call(
        functools.partial(_matmul_kernel, has_resid=resid is not None, emit_bf16=emit_bf16,
                          transposed_w=transposed_w),
        grid=grid,
        in_specs=in_specs,
        out_specs=out_specs,
        out_shape=out_shape,
        scratch_shapes=[pltpu.VMEM((tn, K) if transposed_w else (K, tn), BF16)],
        compiler_params=_cparams(2),
    )(*args)
    return res if emit_bf16 else res[0]


def _rope(x, c, s1, s2, rot):
    half = rot // 2
    return x * c + pltpu.roll(x, half, 1) * s1 + pltpu.roll(x, LANE - half, 1) * s2


def _dsa_post_kernel(z_ref, qg_ref, kg_ref, c_ref, s1_ref, s2_ref,
                     q_ref, kf_ref, kb_ref, vb_ref, qi_ref, kif_ref, kib_ref, wi_ref,
                     kt_ref, kit_ref, *, d):
    c, s1, s2 = c_ref[...], s1_ref[...], s2_ref[...]
    rot = d.hd // 4
    hd = d.hd

    def headnorm(x, g):
        ms = jnp.mean(x * x, axis=-1, keepdims=True)
        return x * lax.rsqrt(ms + EPS) * g

    qg, kg = qg_ref[...], kg_ref[...]
    for h in range(d.n_heads):
        x = z_ref[:, h * hd:(h + 1) * hd]
        q_ref[:, h * hd:(h + 1) * hd] = _rope(headnorm(x, qg), c, s1, s2, rot).astype(BF16)
    off = d.QW
    for h in range(d.n_kv):
        x = z_ref[:, off + h * hd: off + (h + 1) * hd]
        y = _rope(headnorm(x, kg), c, s1, s2, rot)
        kf_ref[:, h * hd:(h + 1) * hd] = y
        kb_ref[:, h * hd:(h + 1) * hd] = y.astype(BF16)
        kt_ref[h * hd:(h + 1) * hd, :] = y.T.astype(BF16)
    off = d.QW + d.KVW
    vb_ref[...] = z_ref[:, off: off + d.KVW].astype(BF16)
    off = d.QW + 2 * d.KVW
    for h in range(d.nih):
        x = z_ref[:, off + h * d.idim: off + (h + 1) * d.idim]
        qi_ref[h] = _rope(x, c, s1, s2, rot).astype(BF16)
    off = d.QW + 2 * d.KVW + d.IQW
    y = _rope(z_ref[:, off: off + d.idim], c, s1, s2, rot)
    kif_ref[...] = y
    kib_ref[...] = y.astype(BF16)
    kit_ref[...] = y.T.astype(BF16)
    off = off + d.idim
    wi_ref[...] = z_ref[:, off: off + d.nih] * (d.nih ** -0.5) * (d.idim ** -0.5)


def dsa_post(z, q_gain, k_gain, layer, tabs, d):
    M = d.M
    tr = TAIL
    c, s1, s2 = tabs
    row = lambda w: pl.BlockSpec((tr, w), lambda m: (m, 0))
    gain = pl.BlockSpec((None, 1, d.hd), lambda m: (layer, 0, 0))
    return pl.pallas_call(
        functools.partial(_dsa_post_kernel, d=d),
        grid=(M // tr,),
        in_specs=[row(z.shape[1]), gain, gain, row(LANE), row(LANE), row(LANE)],
        out_specs=[row(d.QW), row(d.KVW), row(d.KVW), row(d.KVW),
                   pl.BlockSpec((d.nih, tr, d.idim), lambda m: (0, m, 0)),
                   row(d.idim), row(d.idim), row(d.nih),
                   pl.BlockSpec((d.KVW, tr), lambda m: (0, m)),
                   pl.BlockSpec((d.idim, tr), lambda m: (0, m))],
        out_shape=[jax.ShapeDtypeStruct((M, d.QW), BF16),
                   jax.ShapeDtypeStruct((M, d.KVW), F32),
                   jax.ShapeDtypeStruct((M, d.KVW), BF16),
                   jax.ShapeDtypeStruct((M, d.KVW), BF16),
                   jax.ShapeDtypeStruct((d.nih, M, d.idim), BF16),
                   jax.ShapeDtypeStruct((M, d.idim), F32),
                   jax.ShapeDtypeStruct((M, d.idim), BF16),
                   jax.ShapeDtypeStruct((M, d.nih), F32),
                   jax.ShapeDtypeStruct((d.KVW, M), BF16),
                   jax.ShapeDtypeStruct((d.idim, M), BF16)],
        compiler_params=_cparams(1),
    )(z, q_gain[:, None, :], k_gain[:, None, :], c, s1, s2)


def _float_key(x):
    bits = lax.bitcast_convert_type(x, I32)
    return bits ^ ((bits >> 31) & jnp.int32(0x7FFFFFFF))


def _kth_largest_key(key, k):
    kf = jnp.float32(k)

    def count_ge(cand):
        return jnp.sum(jnp.where(key >= cand, 1.0, 0.0), axis=1, keepdims=True)

    t0 = jnp.where(count_ge(jnp.int32(0)) >= kf, jnp.int32(0), jnp.int32(INT_MIN))

    def body(it, t):
        cand = t | lax.shift_left(jnp.int32(1), jnp.int32(30) - it)
        return jnp.where(count_ge(cand) >= kf, cand, t)

    return lax.fori_loop(0, 31, body, t0)


TIE_CHUNK = 256


def _topk_mask(key, k, chunks, out_ref):
    R = key.shape[0]
    thr = _kth_largest_key(key, k)
    valid = key > jnp.int32(INT_MIN)
    floor = jnp.maximum(thr, jnp.int32(INT_MIN + 1))
    for c0, c1 in chunks:
        out_ref[:, c0:c1] = jnp.where(key[:, c0:c1] >= floor, 1.0, 0.0)

    above = jnp.where(key > thr, 1.0, 0.0)
    tied = jnp.where(valid & (key == thr), 1.0, 0.0)
    need = jnp.float32(k) - jnp.sum(above, axis=1, keepdims=True)
    excess = jnp.sum(tied, axis=1, keepdims=True) > need

    @pl.when(jnp.max(jnp.where(excess, 1.0, 0.0)) > 0.5)
    def _():
        seen = jnp.zeros((R, 1), F32)
        for c0, c1 in chunks:
            n = c1 - c0
            e = tied[:, c0:c1]
            before = jnp.where(lax.broadcasted_iota(I32, (n, n), 0)
                               < lax.broadcasted_iota(I32, (n, n), 1), 1.0, 0.0).astype(BF16)
            rows = e if R % 16 == 0 else jnp.concatenate([e, jnp.zeros((-R % 16, n), F32)], axis=0)
            rank = seen + _dot(rows.astype(BF16), before)[:R]
            keep = above[:, c0:c1] + e * jnp.where(rank < need, 1.0, 0.0)
            out_ref[:, c0:c1] = keep
            seen = seen + jnp.sum(e, axis=1, keepdims=True)


def _stack_heads(x, n, w):
    return jnp.concatenate([x[:, j * w:(j + 1) * w] for j in range(n)], axis=0)


def _prompt_query_meta(s, d):
    is_tail = s >= d.B * d.nqb
    b = jnp.minimum(s // d.nqb, d.B - 1)
    i = s % d.nqb
    r = lax.broadcasted_iota(I32, (TAIL, 1), 0)
    q_seq = jnp.where(is_tail, jnp.where(r < d.soff, _div(r, d.n_meta), -1), b)
    q_pos = jnp.where(is_tail, _mod(r, d.n_meta), d.n_meta + i * TAIL + r)
    return is_tail, b, i, q_seq, q_pos


def _tail_key_meta(d):
    c = lax.broadcasted_iota(I32, (1, TAIL), 1)
    k_seq = jnp.where(c < d.soff, _div(c, d.n_meta), -2)
    k_pos = _mod(c, d.n_meta)
    return k_seq, k_pos


DSA_EXTENT_STEP = 4
IDX_KEY_CHUNK = 256
DSA_GROUPS_PER_STEP = 2


def _dsa_prompt_kernel(qi_ref, wi_ref, kip_ref, kit_ref, q_ref, kp_ref, kt_ref, vp_ref, vt_ref,
                       o_ref, score_ref, sel_ref, wb_ref, *, d, n_sel, gp):
    s = pl.program_id(0)
    g = pl.program_id(1)
    _, b, i, q_seq, q_pos = _prompt_query_meta(s, d)

    def body(npk):
        W = npk + TAIL

        @pl.when(g == 0)
        def _():
            cp = lax.broadcasted_iota(I32, (1, npk), 1)
            valid_p = (q_seq == b) & (d.n_meta + cp <= q_pos)
            tk_seq, tk_pos = _tail_key_meta(d)
            valid_t = (tk_seq == q_seq) & (tk_pos <= q_pos)
            valid = jnp.concatenate([valid_p, valid_t], axis=1)

            wi = wi_ref[...]
            for h in range(d.nih):
                wb_ref[h * TAIL:(h + 1) * TAIL, :] = jnp.broadcast_to(wi[:, h:h + 1], (TAIL, LANE))
            qall = qi_ref[...].reshape(d.nih * TAIL, d.idim)

            def chunk_scores(kchunk_t):
                n = kchunk_t.shape[1]
                w = jnp.maximum(_dot(qall, kchunk_t), 0.0)
                w = w * jnp.concatenate([wb_ref[...]] * (n // LANE), axis=1)
                return jnp.sum(w.reshape(d.nih, TAIL, n), axis=0)

            for c0 in range(0, npk, IDX_KEY_CHUNK):
                c1 = min(npk, c0 + IDX_KEY_CHUNK)
                score_ref[:, c0:c1] = chunk_scores(kip_ref[:, c0:c1])
            score_ref[:, npk:W] = chunk_scores(kit_ref[...])
            key = jnp.where(valid, _float_key(score_ref[:, :W]), jnp.int32(INT_MIN))
            chunks = [(npk, W)] + [(c0, min(npk, c0 + TIE_CHUNK)) for c0 in range(0, npk, TIE_CHUNK)]
            _topk_mask(key, n_sel, chunks, sel_ref)

        G = d.G
        RG = G * TAIL
        hd = d.hd
        qg = _stack_heads(q_ref[...], G * gp, hd)
        sc = jnp.concatenate([
            jnp.concatenate([_dot(qg[c * RG:(c + 1) * RG], kp_ref[c * hd:(c + 1) * hd, :npk]),
                             _dot(qg[c * RG:(c + 1) * RG], kt_ref[c * hd:(c + 1) * hd, :])], axis=1)
            for c in range(gp)], axis=0) * d.scale
        sel = jnp.concatenate([sel_ref[:, :W]] * (G * gp), axis=0) > 0.5
        sc = jnp.where(sel, sc, NEG)
        m = jnp.max(sc, axis=1, keepdims=True)
        p = jnp.exp(sc - m)
        l = jnp.sum(p, axis=1, keepdims=True)
        pb = p.astype(BF16)
        o = jnp.concatenate([
            _dot(pb[c * RG:(c + 1) * RG, :npk], vp_ref[:npk, c * hd:(c + 1) * hd])
            + _dot(pb[c * RG:(c + 1) * RG, npk:], vt_ref[:, c * hd:(c + 1) * hd])
            for c in range(gp)], axis=0) / l
        for j in range(G * gp):
            o_ref[:, j * hd:(j + 1) * hd] = o[j * TAIL:(j + 1) * TAIL].astype(o_ref.dtype)

    extents = sorted({min(d.nqb, e) for e in range(DSA_EXTENT_STEP, d.nqb + DSA_EXTENT_STEP,
                                                    DSA_EXTENT_STEP)})
    for lo, ext in zip([0] + extents[:-1], extents):
        pl.when((i >= lo) & (i < ext))(functools.partial(body, ext * TAIL))


def dsa_prompt_attention(q, kt, vb, qi, kit, wi, d):
    n_sel = min(TOPK_MAX, (d.SEQ + d.n_meta) // 4)
    gp = DSA_GROUPS_PER_STEP if d.n_kv % DSA_GROUPS_PER_STEP == 0 else 1
    nblk = d.B * d.nqb + 1
    tailb = d.NP // TAIL
    gw = gp * d.G * d.hd
    kw = gp * d.hd
    seq_of = lambda s: jnp.minimum(s // d.nqb, d.B - 1)
    in_specs = [
        pl.BlockSpec((d.nih, TAIL, d.idim), lambda s, g: (0, s, 0)),
        pl.BlockSpec((TAIL, d.nih), lambda s, g: (s, 0)),
        pl.BlockSpec((d.idim, d.SEQ), lambda s, g: (0, seq_of(s))),
        pl.BlockSpec((d.idim, TAIL), lambda s, g: (0, tailb)),
        pl.BlockSpec((TAIL, gw), lambda s, g: (s, g)),
        pl.BlockSpec((kw, d.SEQ), lambda s, g: (g, seq_of(s))),
        pl.BlockSpec((kw, TAIL), lambda s, g: (g, tailb)),
        pl.BlockSpec((d.SEQ, kw), lambda s, g: (seq_of(s), g)),
        pl.BlockSpec((TAIL, kw), lambda s, g: (tailb, g)),
    ]
    return pl.pallas_call(
        functools.partial(_dsa_prompt_kernel, d=d, n_sel=n_sel, gp=gp),
        grid=(nblk, d.n_kv // gp),
        in_specs=in_specs,
        out_specs=pl.BlockSpec((TAIL, gw), lambda s, g: (s, g)),
        out_shape=jax.ShapeDtypeStruct((d.M, d.QW), BF16),
        scratch_shapes=[pltpu.VMEM((TAIL, d.SEQ + TAIL), F32),
                        pltpu.VMEM((TAIL, d.SEQ + TAIL), F32),
                        pltpu.VMEM((d.nih * TAIL, LANE), F32)],
        compiler_params=_cparams(2),
    )(qi, wi, kit, kit, q, kt, kt, vb, vb)


SB_UNDERFLOW = 104.0


def _log_sigmoid(z):
    return jnp.minimum(z, 0.0) - jnp.log(1.0 + jnp.exp(-jnp.abs(z)))


def _sb_weights(z, valid, carry):
    ls = _log_sigmoid(z)
    lr = ls - z
    if valid is not None:
        lr = jnp.where(valid, lr, 0.0)
    ci = lax.broadcasted_iota(I32, (LANE, LANE), 0)
    cj = lax.broadcasted_iota(I32, (LANE, LANE), 1)
    upper = jnp.where(ci > cj, 1.0, 0.0).astype(BF16)
    hi = lr.astype(BF16)
    lo = (lr - hi.astype(F32)).astype(BF16)
    sfx = _dot(hi, upper) + _dot(lo, upper)
    a = jnp.exp(ls + sfx + carry)
    if valid is not None:
        a = jnp.where(valid, a, 0.0)
    return a.astype(BF16), jnp.sum(lr, axis=1, keepdims=True)


SB_GROUPS_PER_STEP = 8


def _sb_prompt_kernel(q_ref, kp_ref, kt_ref, vp_ref, vt_ref, o_ref, acc_ref, run_ref, *, d, gp):
    s = pl.program_id(0)
    G = d.G
    RG = G * TAIL
    _, _, i, q_seq, q_pos = _prompt_query_meta(s, d)
    q_seq = jnp.concatenate([q_seq] * (G * gp), axis=0)
    q_pos = jnp.concatenate([q_pos] * (G * gp), axis=0)
    qg = _stack_heads(q_ref[...], G * gp, d.hd)
    cl = lax.broadcasted_iota(I32, (1, TAIL), 1)

    def weighted(kblk, vblk, valid, run):
        z = jnp.concatenate(
            [_dot_nt(qg[c * RG:(c + 1) * RG], kblk[:, c * d.hd:(c + 1) * d.hd]) for c in range(gp)],
            axis=0) * d.scale
        a, tot = _sb_weights(z, valid, run)
        o = jnp.concatenate(
            [_dot(a[c * RG:(c + 1) * RG], vblk[:, c * d.hd:(c + 1) * d.hd]) for c in range(gp)],
            axis=0)
        return o, tot

    def block(kb, valid, run):
        start = pl.multiple_of(kb * TAIL, TAIL)
        return weighted(kp_ref[pl.ds(start, TAIL), :], vp_ref[pl.ds(start, TAIL), :], valid, run)

    o, tot = block(i, (d.n_meta + i * TAIL + cl) < q_pos, 0.0)
    acc_ref[...] = o
    run_ref[...] = tot

    def cond(c):
        kb, top = c
        return (kb >= 0) & (top > -SB_UNDERFLOW)

    def body(c):
        kb, _ = c
        o, tot = block(kb, None, run_ref[...])
        acc_ref[...] += o
        run = run_ref[...] + tot
        run_ref[...] = run
        return kb - 1, jnp.max(run)

    _, top = lax.while_loop(cond, body, (i - 1, jnp.max(tot)))

    @pl.when(top > -SB_UNDERFLOW)
    def _():
        tk_seq, tk_pos = _tail_key_meta(d)
        valid = (tk_seq == q_seq) & (tk_pos < q_pos)
        o, _ = weighted(kt_ref[...], vt_ref[...], valid, run_ref[...])
        acc_ref[...] += o

    for j in range(G * gp):
        o_ref[:, j * d.hd:(j + 1) * d.hd] = acc_ref[j * TAIL:(j + 1) * TAIL, :].astype(o_ref.dtype)


def sb_prompt_attention(zb, k_off, v_off, d):
    gp = SB_GROUPS_PER_STEP if d.n_kv % SB_GROUPS_PER_STEP == 0 else 1
    nblk = d.B * d.nqb + 1
    tailb = d.NP // TAIL
    gw = gp * d.G * d.hd
    kw = gp * d.hd
    assert k_off % kw == 0 and v_off % kw == 0
    kblk, vblk = k_off // kw, v_off // kw
    seq_of = lambda s: jnp.minimum(s // d.nqb, d.B - 1)
    in_specs = [
        pl.BlockSpec((TAIL, gw), lambda s, g: (s, g)),
        pl.BlockSpec((d.SEQ, kw), lambda s, g: (seq_of(s), kblk + g)),
        pl.BlockSpec((TAIL, kw), lambda s, g: (tailb, kblk + g)),
        pl.BlockSpec((d.SEQ, kw), lambda s, g: (seq_of(s), vblk + g)),
        pl.BlockSpec((TAIL, kw), lambda s, g: (tailb, vblk + g)),
    ]
    rows = gp * d.G * TAIL
    return pl.pallas_call(
        functools.partial(_sb_prompt_kernel, d=d, gp=gp),
        grid=(nblk, d.n_kv // gp),
        in_specs=in_specs,
        out_specs=pl.BlockSpec((TAIL, gw), lambda s, g: (s, g)),
        out_shape=jax.ShapeDtypeStruct((d.M, d.QW), BF16),
        scratch_shapes=[pltpu.VMEM((rows, d.hd), F32), pltpu.VMEM((rows, 1), F32)],
        compiler_params=_cparams(2),
    )(zb, zb, zb, zb, zb)


def _block_diag_q(q, d):
    R = q.shape[0]
    rho = lax.broadcasted_iota(I32, (R, 1), 0)
    grp = _div(_mod(rho, d.n_heads), d.G)
    zero = jnp.zeros_like(q)
    return jnp.concatenate([jnp.where(grp == g, q, zero) for g in range(d.n_kv)], axis=1)


def _own_group_slab(oall, d):
    R = oall.shape[0]
    rho = lax.broadcasted_iota(I32, (R, 1), 0)
    grp = _div(_mod(rho, d.n_heads), d.G)
    out = jnp.zeros((R, d.hd), F32)
    for g in range(d.n_kv):
        out = out + jnp.where(grp == g, oall[:, g * d.hd:(g + 1) * d.hd], 0.0)
    return out


def _page_rows(page_ref, d):
    return jnp.concatenate([page_ref[:, h, :] for h in range(d.n_kv)], axis=1).astype(BF16)


IDX_PAGES_PER_STEP = 8
ATTN_PAGES_PER_STEP = 8


def _pages_per_step(n_pages, target):
    return max(g for g in range(1, target + 1) if n_pages % g == 0)


def _page_index_map(layer, j, pg, npg, trailing):
    def index_map(b, p, pt):
        return (layer, pt[b, jnp.minimum(p * pg + j, npg - 1)]) + (0,) * trailing
    return index_map


def _sample_idx_kernel(pt_ref, qi_ref, wi_ref, *refs, d, pg):
    kc_refs, kt_ref, o_ref = refs[:pg], refs[pg], refs[pg + 1]
    p = pl.program_id(1)
    last = d.n_pages // pg

    def scores(kblk):
        w = jnp.maximum(_dot_nt(qi_ref[...], kblk), 0.0) * wi_ref[...]
        rows = [jnp.sum(w[t * d.nih:(t + 1) * d.nih], axis=0, keepdims=True) for t in range(d.Ts)]
        rows.append(jnp.zeros((SUBLANE - d.Ts, kblk.shape[0]), F32))
        return jnp.concatenate(rows, axis=0)

    @pl.when(p < last)
    def _():
        o_ref[...] = scores(jnp.concatenate([r[...].astype(BF16) for r in kc_refs], axis=0))

    @pl.when(p == last)
    def _():
        o_ref[...] = jnp.zeros_like(o_ref)
        o_ref[:, :TAIL] = scores(kt_ref[...])


def sample_idx_scores(page_table, qi_s, wi_s, cache_ki, layer, kib, d):
    R = d.Ts * d.nih
    npg = d.n_pages
    pg = _pages_per_step(npg, IDX_PAGES_PER_STEP)
    tailb = d.NP // TAIL
    grid_spec = pltpu.PrefetchScalarGridSpec(
        num_scalar_prefetch=1,
        grid=(d.Bd, npg // pg + 1),
        in_specs=[
            pl.BlockSpec((None, R, d.idim), lambda b, p, pt: (b, 0, 0)),
            pl.BlockSpec((None, R, 1), lambda b, p, pt: (b, 0, 0)),
        ] + [
            pl.BlockSpec((None, None, d.page, d.idim), _page_index_map(layer, j, pg, npg, 2))
            for j in range(pg)
        ] + [
            pl.BlockSpec((TAIL, d.idim), lambda b, p, pt: (tailb, 0)),
        ],
        out_specs=pl.BlockSpec((None, SUBLANE, pg * LANE), lambda b, p, pt: (b, 0, p)),
    )
    return pl.pallas_call(
        functools.partial(_sample_idx_kernel, d=d, pg=pg),
        grid_spec=grid_spec,
        out_shape=jax.ShapeDtypeStruct((d.Bd, SUBLANE, (npg + pg) * LANE), F32),
        compiler_params=_cparams(2),
    )(page_table, qi_s, wi_s, *([cache_ki] * pg), kib)


def _sample_select_kernel(s_ref, o_ref, *, d, n_sel):
    b = pl.program_id(0)
    L = s_ref.shape[1]
    col = lax.broadcasted_iota(I32, (1, L), 1)
    t = lax.broadcasted_iota(I32, (SUBLANE, 1), 0)
    snew = col - d.past - d.soff - d.Ts * b
    valid = (col < d.past) | ((snew >= 0) & (snew < d.Ts) & (snew <= t))
    valid = valid & (t < d.Ts)
    key = jnp.where(valid, _float_key(s_ref[...]), jnp.int32(INT_MIN))
    _topk_mask(key, n_sel, [(c0, min(L, c0 + TIE_CHUNK)) for c0 in range(0, L, TIE_CHUNK)], o_ref)


def sample_select(scores, d):
    n_sel = min(TOPK_MAX, (d.past + d.Ts) // 4)
    L = scores.shape[2]
    spec = pl.BlockSpec((None, SUBLANE, L), lambda b: (b, 0, 0))
    return pl.pallas_call(
        functools.partial(_sample_select_kernel, d=d, n_sel=n_sel),
        grid=(d.Bd,),
        in_specs=[spec],
        out_specs=spec,
        out_shape=jax.ShapeDtypeStruct(scores.shape, F32),
        compiler_params=_cparams(1),
    )(scores)


def _sample_dsa_attn_kernel(pt_ref, q_ref, sel_ref, *refs, d, pg):
    kc_refs, vc_refs = refs[:pg], refs[pg:2 * pg]
    kt_ref, vt_ref, o_ref, expand_ref, own_ref, m_ref, l_ref, acc_ref = refs[2 * pg:]
    p = pl.program_id(1)
    last = d.n_pages // pg
    R = q_ref.shape[0]
    W = d.page * d.n_kv

    @pl.when(p == 0)
    def _():
        key = lax.broadcasted_iota(I32, (d.page, W), 0)
        col = lax.broadcasted_iota(I32, (d.page, W), 1)
        expand_ref[...] = jnp.where(_div(col, d.n_kv) == key, 1.0, 0.0).astype(BF16)
        rho = lax.broadcasted_iota(I32, (R, W), 0)
        col = lax.broadcasted_iota(I32, (R, W), 1)
        own_ref[...] = jnp.where(_mod(col, d.n_kv) == _div(_mod(rho, d.n_heads), d.G), 1.0, 0.0)
        m_ref[...] = jnp.full_like(m_ref, NEG)
        l_ref[...] = jnp.zeros_like(l_ref)
        acc_ref[...] = jnp.zeros_like(acc_ref)

    def per_row(sel):
        return jnp.concatenate(
            [jnp.broadcast_to(sel[t:t + 1], (d.n_heads, sel.shape[1])) for t in range(d.Ts)], axis=0)

    def update(sc, mask, pv):
        sc = jnp.where(mask, sc, NEG)
        m_old = m_ref[...]
        m_new = jnp.maximum(m_old, jnp.max(sc, axis=1, keepdims=True))
        alpha = jnp.exp(m_old - m_new)
        pr = jnp.where(mask, jnp.exp(sc - m_new), 0.0)
        l_ref[...] = alpha * l_ref[...] + jnp.sum(pr, axis=1, keepdims=True)
        acc_ref[...] = alpha * acc_ref[...] + pv(pr.astype(BF16))
        m_ref[...] = m_new

    @pl.when(p < last)
    def _():
        kflat = jnp.concatenate([r[...].reshape(W, d.hd) for r in kc_refs], axis=0).astype(BF16)
        vflat = jnp.concatenate([r[...].reshape(W, d.hd) for r in vc_refs], axis=0).astype(BF16)
        sc = _dot_nt(q_ref[...], kflat) * d.scale
        sel = sel_ref[...].astype(BF16)
        selx = jnp.concatenate([_dot(sel[:, j * d.page:(j + 1) * d.page], expand_ref[...])
                                for j in range(pg)], axis=1)
        own = jnp.concatenate([own_ref[...]] * pg, axis=1)
        update(sc, per_row(selx) * own > 0.5, lambda pr: _dot(pr, vflat))

    @pl.when(p == last)
    def _():
        sc = _dot_nt(_block_diag_q(q_ref[...], d), kt_ref[...]) * d.scale
        update(sc, per_row(sel_ref[:, :TAIL]) > 0.5,
               lambda pr: _own_group_slab(_dot(pr, vt_ref[...]), d))
        o_ref[...] = (acc_ref[...] / l_ref[...]).astype(o_ref.dtype)


def sample_dsa_attention(page_table, q_s, sel, cache_k, cache_v, layer, kb, vb, d):
    R = d.Ts * d.n_heads
    npg = d.n_pages
    pg = _pages_per_step(npg, ATTN_PAGES_PER_STEP)
    assert sel.shape[2] >= (npg + pg) * LANE
    tailb = d.NP // TAIL
    page_specs = [pl.BlockSpec((None, None, d.page, d.n_kv, d.hd),
                               _page_index_map(layer, j, pg, npg, 3)) for j in range(pg)]
    tail_spec = pl.BlockSpec((TAIL, d.KVW), lambda b, p, pt: (tailb, 0))
    grid_spec = pltpu.PrefetchScalarGridSpec(
        num_scalar_prefetch=1,
        grid=(d.Bd, npg // pg + 1),
        in_specs=[
            pl.BlockSpec((None, R, d.hd), lambda b, p, pt: (b, 0, 0)),
            pl.BlockSpec((None, SUBLANE, pg * LANE), lambda b, p, pt: (b, 0, p)),
        ] + page_specs + page_specs + [tail_spec, tail_spec],
        out_specs=pl.BlockSpec((None, R, d.hd), lambda b, p, pt: (b, 0, 0)),
        scratch_shapes=[pltpu.VMEM((d.page, d.page * d.n_kv), BF16),
                        pltpu.VMEM((R, d.page * d.n_kv), F32), pltpu.VMEM((R, 1), F32),
                        pltpu.VMEM((R, 1), F32), pltpu.VMEM((R, d.hd), F32)],
    )
    return pl.pallas_call(
        functools.partial(_sample_dsa_attn_kernel, d=d, pg=pg),
        grid_spec=grid_spec,
        out_shape=jax.ShapeDtypeStruct((d.Bd, R, d.hd), BF16),
        compiler_params=_cparams(2),
    )(page_table, q_s, sel, *([cache_k] * pg), *([cache_v] * pg), kb, vb)


def _sample_sb_kernel(pt_ref, q_ref, kt_ref, vt_ref, kc_hbm, vc_hbm, o_ref,
                      kbuf, vbuf, sems, *, d, layer):
    b = pl.program_id(0)
    R = q_ref.shape[0]
    qbd = _block_diag_q(q_ref[...], d)

    def block(kblk, vblk, valid, run):
        a, tot = _sb_weights(_dot_nt(qbd, kblk) * d.scale, valid, run)
        return _own_group_slab(_dot(a, vblk), d), tot

    rho = lax.broadcasted_iota(I32, (R, 1), 0)
    tq = _div(rho, d.n_heads)
    c = lax.broadcasted_iota(I32, (1, TAIL), 1)
    snew = c - d.soff - d.Ts * b
    acc0, run0 = block(kt_ref[...], vt_ref[...], (snew >= 0) & (snew < tq), 0.0)

    def page_copies(page):
        return (pltpu.make_async_copy(kc_hbm.at[layer, page], kbuf, sems.at[0]),
                pltpu.make_async_copy(vc_hbm.at[layer, page], vbuf, sems.at[1]))

    def cond(carry):
        p, top, _, _ = carry
        return (p >= 0) & (top > -SB_UNDERFLOW)

    def body(carry):
        p, _, acc, run = carry
        copies = page_copies(pt_ref[b, p])
        for cp in copies:
            cp.start()
        for cp in copies:
            cp.wait()
        od, tot = block(_page_rows(kbuf, d), _page_rows(vbuf, d), None, run)
        run = run + tot
        return p - 1, jnp.max(run), acc + od, run

    _, _, acc, _ = lax.while_loop(cond, body,
                                  (jnp.int32(d.n_pages - 1), jnp.max(run0), acc0, run0))
    o_ref[...] = acc.astype(o_ref.dtype)


def sample_sb_attention(page_table, q_s, cache_k, cache_v, layer, zb, k_off, v_off, d):
    R = d.Ts * d.n_heads
    tailb = d.NP // TAIL
    page_shape = (d.page, d.n_kv, d.hd)
    grid_spec = pltpu.PrefetchScalarGridSpec(
        num_scalar_prefetch=1,
        grid=(d.Bd,),
        in_specs=[
            pl.BlockSpec((None, R, d.hd), lambda b, pt: (b, 0, 0)),
            pl.BlockSpec((TAIL, d.KVW), lambda b, pt: (tailb, k_off)),
            pl.BlockSpec((TAIL, d.KVW), lambda b, pt: (tailb, v_off)),
            pl.BlockSpec(memory_space=pl.ANY),
            pl.BlockSpec(memory_space=pl.ANY),
        ],
        out_specs=pl.BlockSpec((None, R, d.hd), lambda b, pt: (b, 0, 0)),
        scratch_shapes=[pltpu.VMEM(page_shape, F32), pltpu.VMEM(page_shape, F32),
                        pltpu.SemaphoreType.DMA((2,))],
    )
    return pl.pallas_call(
        functools.partial(_sample_sb_kernel, d=d, layer=layer),
        grid_spec=grid_spec,
        out_shape=jax.ShapeDtypeStruct((d.Bd, R, d.hd), BF16),
        compiler_params=_cparams(1),
    )(page_table, q_s, zb, zb, cache_k, cache_v)


def _conv_glu_kernel(g_ref, halo_ref, u_ref, w_ref, b_ref, p1_ref, p2_ref, o_ref, *, d):
    m = pl.program_id(1)
    is_tail = m == d.NP // TAIL
    g = g_ref[...]
    gext = jnp.concatenate([halo_ref[...], g], axis=0)
    g1 = pltpu.roll(gext, 1, 0)[SUBLANE:]
    g2 = pltpu.roll(gext, 2, 0)[SUBLANE:]
    r = lax.broadcasted_iota(I32, (TAIL, 1), 0)
    step = jnp.where(r < d.soff, _mod(r, d.n_meta), _mod(r - d.soff, d.Ts))
    g1 = jnp.where(is_tail & (step < 1), p1_ref[...], g1)
    g2 = jnp.where(is_tail & (step < 2), p2_ref[...], g2)
    gc = b_ref[...] + g2 * w_ref[0:1, :] + g1 * w_ref[1:2, :] + g * w_ref[2:3, :]
    o_ref[...] = (gc * jax.nn.sigmoid(gc) * u_ref[...]).astype(o_ref.dtype)


def conv_glu_gate(gu, conv_w, conv_b, layer, prev1, prev2, d):
    assert conv_w.shape[1] == 3
    M = d.M
    tf = _pick_tile(d.d_ff, 5504, LANE)
    nf = d.d_ff // tf
    per_seq = d.nqb
    tail_m = d.NP // TAIL

    def halo_idx(f, m):
        b = m // per_seq
        seq_start = (d.NP + b * d.n_meta + d.n_meta - SUBLANE) // SUBLANE
        normal = (TAIL // SUBLANE) * m - 1
        idx = jnp.where(m % per_seq == 0, seq_start, normal)
        return (jnp.where(m == tail_m, 0, idx), f)

    return pl.pallas_call(
        functools.partial(_conv_glu_kernel, d=d),
        grid=(nf, M // TAIL),
        in_specs=[
            pl.BlockSpec((TAIL, tf), lambda f, m: (m, f)),
            pl.BlockSpec((SUBLANE, tf), halo_idx),
            pl.BlockSpec((TAIL, tf), lambda f, m: (m, nf + f)),
            pl.BlockSpec((None, 3, tf), lambda f, m: (layer, 0, f)),
            pl.BlockSpec((None, 1, tf), lambda f, m: (layer, 0, f)),
            pl.BlockSpec((TAIL, tf), lambda f, m: (0, f)),
            pl.BlockSpec((TAIL, tf), lambda f, m: (0, f)),
        ],
        out_specs=pl.BlockSpec((TAIL, tf), lambda f, m: (m, f)),
        out_shape=jax.ShapeDtypeStruct((M, d.d_ff), BF16),
        compiler_params=_cparams(2),
    )(gu, gu, gu, conv_w, conv_b[:, None, :], prev1, prev2)


def _rope_tables(pos, hd):
    rot = hd // 4
    half = rot // 2
    inv = ROPE_THETA ** (-jnp.arange(half, dtype=F32) / half)
    ang = pos.astype(F32)[:, None] * inv[None, :]
    cos, sin = jnp.cos(ang), jnp.sin(ang)
    n = pos.shape[0]
    ones = jnp.ones((n, hd - rot), F32)
    zeros = jnp.zeros((n, hd - rot), F32)
    zh = jnp.zeros((n, half), F32)
    c = jnp.concatenate([cos, cos, ones], axis=1)
    s1 = jnp.concatenate([zh, sin, zeros], axis=1)
    s2 = jnp.concatenate([-sin, zh, zeros], axis=1)
    return c, s1, s2


def _to_seq(rows, d):
    W = rows.shape[1]
    meta = rows[d.NP: d.NP + d.soff].reshape(d.B, d.n_meta, W)
    prm = rows[: d.NP].reshape(d.B, d.SEQ, W)
    smp = rows[d.NP + d.soff: d.NP + d.soff + d.Bd * d.Ts].reshape(d.Bd, d.Ts, W)
    return jnp.concatenate([meta, prm], axis=1), smp


def _sample_rows(x, d, per_row):
    w = x.shape[1] // per_row
    lo = d.NP + d.soff
    return x[lo: lo + d.Bd * d.Ts].reshape(d.Bd, d.Ts * per_row, w)


def _with_sample_rows(o, o_s, d):
    return lax.dynamic_update_slice(o, o_s.reshape(d.Bd * d.Ts, d.QW), (d.NP + d.soff, 0))


def kernel(x_prompt, x_sample, cache_k_a, cache_v_a, cache_kidx_a, cache_k_b, cache_v_b,
           state_ffn_conv, page_table, meta_tokens, norm_mix, norm_ffn, w_in_a, q_norm_a,
           k_norm_a, w_out_a, w_in_b, w_out_b, w_gate_up, conv_w, conv_b, w_down):
    B, SEQ, D = x_prompt.shape
    Bd, Ts, _ = x_sample.shape
    n_meta = meta_tokens.shape[0]
    _, n_pool, page, n_kv, hd = cache_k_a.shape
    idim = cache_kidx_a.shape[-1]
    QW = w_out_a.shape[1]
    nih = (w_in_a.shape[2] - QW - 2 * n_kv * hd - idim) // idim
    d = Dims(B=B, SEQ=SEQ, D=D, Bd=Bd, Ts=Ts, n_meta=n_meta, page=page,
             n_pages=page_table.shape[1], n_kv=n_kv, hd=hd, n_heads=QW // hd, nih=nih,
             idim=idim, d_ff=w_down.shape[1], depth=norm_mix.shape[0])
    assert SEQ % TAIL == 0 and page == LANE and hd == LANE and idim == LANE
    assert d.soff + Bd * Ts <= TAIL and n_meta % SUBLANE == 0 and 2 <= Ts <= SUBLANE
    assert w_in_a.shape[2] == d.QW + 2 * d.KVW + d.IQW + idim + nih
    assert (Ts * d.nih) % SUBLANE == 0 and d.nih % SUBLANE == 0
    npad = TAIL - d.soff - Bd * Ts

    h = jnp.concatenate([x_prompt.reshape(d.NP, D), jnp.tile(meta_tokens, (B, 1)),
                         x_sample.reshape(Bd * Ts, D), jnp.zeros((npad, D), F32)], axis=0)
    pos = jnp.concatenate([jnp.tile(n_meta + jnp.arange(SEQ), B), jnp.tile(jnp.arange(n_meta), B),
                           jnp.tile(d.past + jnp.arange(Ts), Bd), jnp.zeros((npad,), I32)])
    tabs = _rope_tables(pos, hd)

    outs = {k: [] for k in ("ka_p", "va_p", "ki_p", "kb_p", "vb_p", "cv_p",
                            "ka_s", "va_s", "ki_s", "kb_s", "vb_s", "cv_s")}
    for layer in range(d.depth):
        j = layer // 2
        n = rmsnorm_bf16(h, norm_mix, layer)
        if layer % 2 == 0:
            z = matmul(n, w_in_a, j, transposed_w=True)
            q, kf, kb, vb, qi, kif, kib, wi, kt, kit = dsa_post(z, q_norm_a, k_norm_a, j, tabs, d)
            o = dsa_prompt_attention(q, kt, vb, qi, kit, wi, d)
            qi_rows = jnp.transpose(qi[:, d.NP + d.soff: d.NP + d.soff + Bd * Ts], (1, 0, 2))
            qi_s = qi_rows.reshape(Bd, Ts * d.nih, idim)
            wi_s = _sample_rows(wi, d, d.nih).reshape(Bd, Ts * d.nih, 1)
            scores = sample_idx_scores(page_table, qi_s, wi_s, cache_kidx_a, j, kib, d)
            sel = sample_select(scores, d)
            o_s = sample_dsa_attention(page_table, _sample_rows(q, d, d.n_heads), sel,
                                       cache_k_a, cache_v_a, j, kb, vb, d)
            o = _with_sample_rows(o, o_s, d)
            vf = z[:, d.QW + d.KVW: d.QW + 2 * d.KVW]
            for name, rows in (("ka", kf), ("va", vf), ("ki", kif)):
                p_, s_ = _to_seq(rows, d)
                outs[name + "_p"].append(p_)
                outs[name + "_s"].append(s_)
            w_out = w_out_a
        else:
            z, zb = matmul(n, w_in_b, j, emit_bf16=True)
            o = sb_prompt_attention(zb, d.QW, d.QW + d.KVW, d)
            lo = d.NP + d.soff
            q_s = zb[lo: lo + Bd * Ts, :d.QW].reshape(Bd, Ts * d.n_heads, hd)
            o_s = sample_sb_attention(page_table, q_s, cache_k_b, cache_v_b, j, zb,
                                      d.QW // d.KVW, d.QW // d.KVW + 1, d)
            o = _with_sample_rows(o, o_s, d)
            for name, lo in (("kb", d.QW), ("vb", d.QW + d.KVW)):
                p_, s_ = _to_seq(z[:, lo: lo + d.KVW], d)
                outs[name + "_p"].append(p_)
                outs[name + "_s"].append(s_)
            w_out = w_out_b
        h = matmul(o, w_out, j, resid=h)

        n = rmsnorm_bf16(h, norm_ffn, layer)
        gu = matmul(n, w_gate_up, layer)
        st = state_ffn_conv[layer]
        zrow = jnp.zeros((Bd, 1, d.d_ff), F32)
        pad_s = jnp.zeros((Bd, Ts - 2, d.d_ff), F32)
        p1 = jnp.concatenate([st[:, 1:2], zrow, pad_s], axis=1).reshape(Bd * Ts, d.d_ff)
        p2 = jnp.concatenate([st[:, 0:1], st[:, 1:2], pad_s], axis=1).reshape(Bd * Ts, d.d_ff)
        top = jnp.zeros((d.soff, d.d_ff), F32)
        bot = jnp.zeros((npad, d.d_ff), F32)
        prev1 = jnp.concatenate([top, p1, bot], axis=0)
        prev2 = jnp.concatenate([top, p2, bot], axis=0)
        act = conv_glu_gate(gu, conv_w, conv_b, layer, prev1, prev2, d)
        h = matmul(act, w_down, layer, k_block=0, k_blocks=2, resid=h)
        h = matmul(act, w_down, layer, k_block=1, k_blocks=2, resid=h)
        lo = d.NP + d.soff
        outs["cv_p"].append(jnp.stack(
            [gu[(b + 1) * SEQ - 2: (b + 1) * SEQ, :d.d_ff] for b in range(B)]))
        outs["cv_s"].append(gu[lo: lo + Bd * Ts, :d.d_ff].reshape(Bd, Ts, d.d_ff)[:, -2:])

    def heads(xs):
        a = jnp.stack(xs)
        return a.reshape(a.shape[:3] + (d.n_kv, d.hd))

    y_p = h[: d.NP].reshape(B, SEQ, D)
    y_s = h[d.NP + d.soff: d.NP + d.soff + Bd * Ts].reshape(Bd, Ts, D)
    return (y_p, y_s,
            heads(outs["ka_p"]), heads(outs["va_p"]), jnp.stack(outs["ki_p"]),
            heads(outs["kb_p"]), heads(outs["vb_p"]), jnp.stack(outs["cv_p"]),
            heads(outs["ka_s"]), heads(outs["va_s"]), jnp.stack(outs["ki_s"]),
            heads(outs["kb_s"]), heads(outs["vb_s"]), jnp.stack(outs["cv_s"]))
```

```python
import functools
from typing import NamedTuple

import jax
import jax.numpy as jnp
from jax import lax
from jax.experimental import pallas as pl
from jax.experimental.pallas import tpu as pltpu

F32 = jnp.float32
BF16 = jnp.bfloat16
I32 = jnp.int32

TOPK_MAX = 256
ROPE_THETA = 500000.0
EPS = 1e-6
LANE = 128
SUBLANE = 8
TAIL = 128
NEG = -1e30
INT_MIN = -2 ** 31
VMEM_LIMIT = 56 * 1024 * 1024


class Dims(NamedTuple):
    B: int
    SEQ: int
    D: int
    Bd: int
    Ts: int
    n_meta: int
    page: int
    n_pages: int
    n_kv: int
    hd: int
    n_heads: int
    nih: int
    idim: int
    d_ff: int
    depth: int

    @property
    def NP(self):
        return self.B * self.SEQ

    @property
    def M(self):
        return self.NP + TAIL

    @property
    def QW(self):
        return self.n_heads * self.hd

    @property
    def KVW(self):
        return self.n_kv * self.hd

    @property
    def IQW(self):
        return self.nih * self.idim

    @property
    def G(self):
        return self.n_heads // self.n_kv

    @property
    def soff(self):
        return self.B * self.n_meta

    @property
    def past(self):
        return self.n_pages * self.page

    @property
    def nqb(self):
        return self.SEQ // TAIL

    @property
    def scale(self):
        return self.hd ** -0.5


def _cparams(n_axes):
    return pltpu.CompilerParams(dimension_semantics=("arbitrary",) * n_axes,
                                vmem_limit_bytes=VMEM_LIMIT)


def _pick_tile(total, target, align):
    best = None
    t = align
    while t <= min(total, target):
        if total % t == 0:
            best = t
        t += align
    assert best is not None, (total, target, align)
    return best


def _div(x, n):
    if n & (n - 1) == 0:
        return x >> (n.bit_length() - 1)
    return x // n


def _mod(x, n):
    if n & (n - 1) == 0:
        return x & (n - 1)
    return x % n


def _dot_nt(a, b):
    return lax.dot_general(a, b, (((1,), (1,)), ((), ())), preferred_element_type=F32)


def _dot(a, b):
    return jnp.dot(a, b, preferred_element_type=F32)


def _rmsnorm_kernel(x_ref, g_ref, o_ref):
    x = x_ref[...]
    ms = jnp.mean(x * x, axis=-1, keepdims=True)
    o_ref[...] = (x * lax.rsqrt(ms + EPS) * g_ref[...]).astype(o_ref.dtype)


def rmsnorm_bf16(h, gains, layer):
    M, D = h.shape
    tr = _pick_tile(M, 512, 16)
    return pl.pallas_call(
        _rmsnorm_kernel,
        grid=(M // tr,),
        in_specs=[pl.BlockSpec((tr, D), lambda m: (m, 0)),
                  pl.BlockSpec((None, 1, D), lambda m: (layer, 0, 0))],
        out_specs=pl.BlockSpec((tr, D), lambda m: (m, 0)),
        out_shape=jax.ShapeDtypeStruct((M, D), BF16),
        compiler_params=_cparams(1),
    )(h, gains[:, None, :])


def _matmul_kernel(*refs, has_resid, emit_bf16, transposed_w):
    x_ref, w_ref = refs[0], refs[1]
    pos = 2
    r_ref = None
    if has_resid:
        r_ref = refs[pos]
        pos += 1
    o_ref = refs[pos]
    pos += 1
    ob_ref = None
    if emit_bf16:
        ob_ref = refs[pos]
        pos += 1
    wb_ref = refs[pos]

    @pl.when(pl.program_id(1) == 0)
    def _():
        wb_ref[...] = w_ref[...].astype(BF16)

    acc = (_dot_nt if transposed_w else _dot)(x_ref[...], wb_ref[...])
    if has_resid:
        acc = acc + r_ref[...]
    o_ref[...] = acc
    if emit_bf16:
        ob_ref[...] = acc.astype(BF16)


def matmul(x, w, layer, *, k_block=0, k_blocks=1, resid=None, emit_bf16=False, n_cols=None,
           transposed_w=False):
    M, Kx = x.shape
    _, Kw, N = w.shape
    N = N if n_cols is None else n_cols
    assert Kx == Kw and Kx % k_blocks == 0
    K = Kx // k_blocks
    assert K % LANE == 0
    tm = _pick_tile(M, 1040 if K <= 4096 else 640, 16)
    tn_target = 512
    tn = tn_target
    for cand in (tn_target, 256, 128):
        if cand <= tn_target and N % cand == 0:
            tn = cand
            break
    grid = (pl.cdiv(N, tn), M // tm)
    in_specs = [pl.BlockSpec((tm, K), lambda n, m: (m, k_block))]
    if transposed_w:
        in_specs.append(pl.BlockSpec((None, tn, K), lambda n, m: (layer, n, k_block)))
        args = [x, jnp.swapaxes(w, 1, 2)]
    else:
        in_specs.append(pl.BlockSpec((None, K, tn), lambda n, m: (layer, k_block, n)))
        args = [x, w]
    if resid is not None:
        in_specs.append(pl.BlockSpec((tm, tn), lambda n, m: (m, n)))
        args.append(resid)
    out_specs = [pl.BlockSpec((tm, tn), lambda n, m: (m, n))]
    out_shape = [jax.ShapeDtypeStruct((M, N), F32)]
    if emit_bf16:
        out_specs.append(pl.BlockSpec((tm, tn), lambda n, m: (m, n)))
        out_shape.append(jax.ShapeDtypeStruct((M, N), BF16))
    res = pl.pallas_call(
        functools.partial(_matmul_kernel, has_resid=resid is not None, emit_bf16=emit_bf16,
                          transposed_w=transposed_w),
        grid=grid,
        in_specs=in_specs,
        out_specs=out_specs,
        out_shape=out_shape,
        scratch_shapes=[pltpu.VMEM((tn, K) if transposed_w else (K, tn), BF16)],
        compiler_params=_cparams(2),
    )(*args)
    return res if emit_bf16 else res[0]


def _rope(x, c, s1, s2, rot):
    half = rot // 2
    return x * c + pltpu.roll(x, half, 1) * s1 + pltpu.roll(x, LANE - half, 1) * s2


def _dsa_post_kernel(z_ref, qg_ref, kg_ref, c_ref, s1_ref, s2_ref,
                     q_ref, kf_ref, kb_ref, vb_ref, qi_ref, kif_ref, kib_ref, wi_ref,
                     kt_ref, kit_ref, *, d):
    c, s1, s2 = c_ref[...], s1_ref[...], s2_ref[...]
    rot = d.hd // 4
    hd = d.hd

    def headnorm(x, g):
        ms = jnp.mean(x * x, axis=-1, keepdims=True)
        return x * lax.rsqrt(ms + EPS) * g

    qg, kg = qg_ref[...], kg_ref[...]
    for h in range(d.n_heads):
        x = z_ref[:, h * hd:(h + 1) * hd]
        q_ref[:, h * hd:(h + 1) * hd] = _rope(headnorm(x, qg), c, s1, s2, rot).astype(BF16)
    off = d.QW
    for h in range(d.n_kv):
        x = z_ref[:, off + h * hd: off + (h + 1) * hd]
        y = _rope(headnorm(x, kg), c, s1, s2, rot)
        kf_ref[:, h * hd:(h + 1) * hd] = y
        kb_ref[:, h * hd:(h + 1) * hd] = y.astype(BF16)
        kt_ref[h * hd:(h + 1) * hd, :] = y.T.astype(BF16)
    off = d.QW + d.KVW
    vb_ref[...] = z_ref[:, off: off + d.KVW].astype(BF16)
    off = d.QW + 2 * d.KVW
    for h in range(d.nih):
        x = z_ref[:, off + h * d.idim: off + (h + 1) * d.idim]
        qi_ref[h] = _rope(x, c, s1, s2, rot).astype(BF16)
    off = d.QW + 2 * d.KVW + d.IQW
    y = _rope(z_ref[:, off: off + d.idim], c, s1, s2, rot)
    kif_ref[...] = y
    kib_ref[...] = y.astype(BF16)
    kit_ref[...] = y.T.astype(BF16)
    off = off + d.idim
    wi_ref[...] = z_ref[:, off: off + d.nih] * (d.nih ** -0.5) * (d.idim ** -0.5)


def dsa_post(z, q_gain, k_gain, layer, tabs, d):
    M = d.M
    tr = TAIL
    c, s1, s2 = tabs
    row = lambda w: pl.BlockSpec((tr, w), lambda m: (m, 0))
    gain = pl.BlockSpec((None, 1, d.hd), lambda m: (layer, 0, 0))
    return pl.pallas_call(
        functools.partial(_dsa_post_kernel, d=d),
        grid=(M // tr,),
        in_specs=[row(z.shape[1]), gain, gain, row(LANE), row(LANE), row(LANE)],
        out_specs=[row(d.QW), row(d.KVW), row(d.KVW), row(d.KVW),
                   pl.BlockSpec((d.nih, tr, d.idim), lambda m: (0, m, 0)),
                   row(d.idim), row(d.idim), row(d.nih),
                   pl.BlockSpec((d.KVW, tr), lambda m: (0, m)),
                   pl.BlockSpec((d.idim, tr), lambda m: (0, m))],
        out_shape=[jax.ShapeDtypeStruct((M, d.QW), BF16),
                   jax.ShapeDtypeStruct((M, d.KVW), F32),
                   jax.ShapeDtypeStruct((M, d.KVW), BF16),
                   jax.ShapeDtypeStruct((M, d.KVW), BF16),
                   jax.ShapeDtypeStruct((d.nih, M, d.idim), BF16),
                   jax.ShapeDtypeStruct((M, d.idim), F32),
                   jax.ShapeDtypeStruct((M, d.idim), BF16),
                   jax.ShapeDtypeStruct((M, d.nih), F32),
                   jax.ShapeDtypeStruct((d.KVW, M), BF16),
                   jax.ShapeDtypeStruct((d.idim, M), BF16)],
        compiler_params=_cparams(1),
    )(z, q_gain[:, None, :], k_gain[:, None, :], c, s1, s2)


def _float_key(x):
    bits = lax.bitcast_convert_type(x, I32)
    return bits ^ ((bits >> 31) & jnp.int32(0x7FFFFFFF))


def _kth_largest_key(key, k):
    kf = jnp.float32(k)

    def count_ge(cand):
        return jnp.sum(jnp.where(key >= cand, 1.0, 0.0), axis=1, keepdims=True)

    t0 = jnp.where(count_ge(jnp.int32(0)) >= kf, jnp.int32(0), jnp.int32(INT_MIN))

    def body(it, t):
        cand = t | lax.shift_left(jnp.int32(1), jnp.int32(30) - it)
        return jnp.where(count_ge(cand) >= kf, cand, t)

    return lax.fori_loop(0, 31, body, t0)


TIE_CHUNK = 256


def _topk_mask(key, k, chunks, out_ref):
    R = key.shape[0]
    thr = _kth_largest_key(key, k)
    valid = key > jnp.int32(INT_MIN)
    floor = jnp.maximum(thr, jnp.int32(INT_MIN + 1))
    for c0, c1 in chunks:
        out_ref[:, c0:c1] = jnp.where(key[:, c0:c1] >= floor, 1.0, 0.0)

    above = jnp.where(key > thr, 1.0, 0.0)
    tied = jnp.where(valid & (key == thr), 1.0, 0.0)
    need = jnp.float32(k) - jnp.sum(above, axis=1, keepdims=True)
    excess = jnp.sum(tied, axis=1, keepdims=True) > need

    @pl.when(jnp.max(jnp.where(excess, 1.0, 0.0)) > 0.5)
    def _():
        seen = jnp.zeros((R, 1), F32)
        for c0, c1 in chunks:
            n = c1 - c0
            e = tied[:, c0:c1]
            before = jnp.where(lax.broadcasted_iota(I32, (n, n), 0)
                               < lax.broadcasted_iota(I32, (n, n), 1), 1.0, 0.0).astype(BF16)
            rows = e if R % 16 == 0 else jnp.concatenate([e, jnp.zeros((-R % 16, n), F32)], axis=0)
            rank = seen + _dot(rows.astype(BF16), before)[:R]
            keep = above[:, c0:c1] + e * jnp.where(rank < need, 1.0, 0.0)
            out_ref[:, c0:c1] = keep
            seen = seen + jnp.sum(e, axis=1, keepdims=True)


def _stack_heads(x, n, w):
    return jnp.concatenate([x[:, j * w:(j + 1) * w] for j in range(n)], axis=0)


def _prompt_query_meta(s, d):
    is_tail = s >= d.B * d.nqb
    b = jnp.minimum(s // d.nqb, d.B - 1)
    i = s % d.nqb
    r = lax.broadcasted_iota(I32, (TAIL, 1), 0)
    q_seq = jnp.where(is_tail, jnp.where(r < d.soff, _div(r, d.n_meta), -1), b)
    q_pos = jnp.where(is_tail, _mod(r, d.n_meta), d.n_meta + i * TAIL + r)
    return is_tail, b, i, q_seq, q_pos


def _tail_key_meta(d):
    c = lax.broadcasted_iota(I32, (1, TAIL), 1)
    k_seq = jnp.where(c < d.soff, _div(c, d.n_meta), -2)
    k_pos = _mod(c, d.n_meta)
    return k_seq, k_pos


DSA_EXTENT_STEP = 2
IDX_KEY_CHUNK = 256
DSA_GROUPS_PER_STEP = 2


def _dsa_prompt_kernel(qi_ref, wi_ref, kip_ref, kit_ref, q_ref, kp_ref, kt_ref, vp_ref, vt_ref,
                       o_ref, score_ref, sel_ref, wb_ref, *, d, n_sel, gp):
    s = pl.program_id(0)
    g = pl.program_id(1)
    _, b, i, q_seq, q_pos = _prompt_query_meta(s, d)

    def body(npk):
        W = npk + TAIL

        @pl.when(g == 0)
        def _():
            cp = lax.broadcasted_iota(I32, (1, npk), 1)
            valid_p = (q_seq == b) & (d.n_meta + cp <= q_pos)
            tk_seq, tk_pos = _tail_key_meta(d)
            valid_t = (tk_seq == q_seq) & (tk_pos <= q_pos)
            valid = jnp.concatenate([valid_p, valid_t], axis=1)

            wi = wi_ref[...]
            for h in range(d.nih):
                wb_ref[h * TAIL:(h + 1) * TAIL, :] = jnp.broadcast_to(wi[:, h:h + 1], (TAIL, LANE))
            qall = qi_ref[...].reshape(d.nih * TAIL, d.idim)

            def chunk_scores(kchunk_t):
                n = kchunk_t.shape[1]
                w = jnp.maximum(_dot(qall, kchunk_t), 0.0)
                w = w * jnp.concatenate([wb_ref[...]] * (n // LANE), axis=1)
                return jnp.sum(w.reshape(d.nih, TAIL, n), axis=0)

            for c0 in range(0, npk, IDX_KEY_CHUNK):
                c1 = min(npk, c0 + IDX_KEY_CHUNK)
                score_ref[:, c0:c1] = chunk_scores(kip_ref[:, c0:c1])
            score_ref[:, npk:W] = chunk_scores(kit_ref[...])
            key = jnp.where(valid, _float_key(score_ref[:, :W]), jnp.int32(INT_MIN))
            chunks = [(npk, W)] + [(c0, min(npk, c0 + TIE_CHUNK)) for c0 in range(0, npk, TIE_CHUNK)]
            _topk_mask(key, n_sel, chunks, sel_ref)

        G = d.G
        RG = G * TAIL
        hd = d.hd
        qg = _stack_heads(q_ref[...], G * gp, hd)
        sc = jnp.concatenate([
            jnp.concatenate([_dot(qg[c * RG:(c + 1) * RG], kp_ref[c * hd:(c + 1) * hd, :npk]),
                             _dot(qg[c * RG:(c + 1) * RG], kt_ref[c * hd:(c + 1) * hd, :])], axis=1)
            for c in range(gp)], axis=0) * d.scale
        sel = jnp.concatenate([sel_ref[:, :W]] * (G * gp), axis=0) > 0.5
        sc = jnp.where(sel, sc, NEG)
        m = jnp.max(sc, axis=1, keepdims=True)
        p = jnp.exp(sc - m)
        l = jnp.sum(p, axis=1, keepdims=True)
        pb = p.astype(BF16)
        o = jnp.concatenate([
            _dot(pb[c * RG:(c + 1) * RG, :npk], vp_ref[:npk, c * hd:(c + 1) * hd])
            + _dot(pb[c * RG:(c + 1) * RG, npk:], vt_ref[:, c * hd:(c + 1) * hd])
            for c in range(gp)], axis=0) / l
        for j in range(G * gp):
            o_ref[:, j * hd:(j + 1) * hd] = o[j * TAIL:(j + 1) * TAIL].astype(o_ref.dtype)

    extents = sorted({min(d.nqb, e) for e in range(DSA_EXTENT_STEP, d.nqb + DSA_EXTENT_STEP,
                                                    DSA_EXTENT_STEP)})
    for lo, ext in zip([0] + extents[:-1], extents):
        pl.when((i >= lo) & (i < ext))(functools.partial(body, ext * TAIL))


def dsa_prompt_attention(q, kt, vb, qi, kit, wi, d):
    n_sel = min(TOPK_MAX, (d.SEQ + d.n_meta) // 4)
    gp = DSA_GROUPS_PER_STEP if d.n_kv % DSA_GROUPS_PER_STEP == 0 else 1
    nblk = d.B * d.nqb + 1
    tailb = d.NP // TAIL
    gw = gp * d.G * d.hd
    kw = gp * d.hd
    seq_of = lambda s: jnp.minimum(s // d.nqb, d.B - 1)
    in_specs = [
        pl.BlockSpec((d.nih, TAIL, d.idim), lambda s, g: (0, s, 0)),
        pl.BlockSpec((TAIL, d.nih), lambda s, g: (s, 0)),
        pl.BlockSpec((d.idim, d.SEQ), lambda s, g: (0, seq_of(s))),
        pl.BlockSpec((d.idim, TAIL), lambda s, g: (0, tailb)),
        pl.BlockSpec((TAIL, gw), lambda s, g: (s, g)),
        pl.BlockSpec((kw, d.SEQ), lambda s, g: (g, seq_of(s))),
        pl.BlockSpec((kw, TAIL), lambda s, g: (g, tailb)),
        pl.BlockSpec((d.SEQ, kw), lambda s, g: (seq_of(s), g)),
        pl.BlockSpec((TAIL, kw), lambda s, g: (tailb, g)),
    ]
    return pl.pallas_call(
        functools.partial(_dsa_prompt_kernel, d=d, n_sel=n_sel, gp=gp),
        grid=(nblk, d.n_kv // gp),
        in_specs=in_specs,
        out_specs=pl.BlockSpec((TAIL, gw), lambda s, g: (s, g)),
        out_shape=jax.ShapeDtypeStruct((d.M, d.QW), BF16),
        scratch_shapes=[pltpu.VMEM((TAIL, d.SEQ + TAIL), F32),
                        pltpu.VMEM((TAIL, d.SEQ + TAIL), F32),
                        pltpu.VMEM((d.nih * TAIL, LANE), F32)],
        compiler_params=_cparams(2),
    )(qi, wi, kit, kit, q, kt, kt, vb, vb)


SB_UNDERFLOW = 104.0


def _log_sigmoid(z):
    return jnp.minimum(z, 0.0) - jnp.log(1.0 + jnp.exp(-jnp.abs(z)))


def _sb_weights(z, valid, carry):
    ls = _log_sigmoid(z)
    lr = ls - z
    if valid is not None:
        lr = jnp.where(valid, lr, 0.0)
    ci = lax.broadcasted_iota(I32, (LANE, LANE), 0)
    cj = lax.broadcasted_iota(I32, (LANE, LANE), 1)
    upper = jnp.where(ci > cj, 1.0, 0.0).astype(BF16)
    hi = lr.astype(BF16)
    lo = (lr - hi.astype(F32)).astype(BF16)
    sfx = _dot(hi, upper) + _dot(lo, upper)
    a = jnp.exp(ls + sfx + carry)
    if valid is not None:
        a = jnp.where(valid, a, 0.0)
    return a.astype(BF16), jnp.sum(lr, axis=1, keepdims=True)


SB_GROUPS_PER_STEP = 8


def _sb_prompt_kernel(q_ref, kp_ref, kt_ref, vp_ref, vt_ref, o_ref, acc_ref, run_ref, *, d, gp):
    s = pl.program_id(0)
    G = d.G
    RG = G * TAIL
    _, _, i, q_seq, q_pos = _prompt_query_meta(s, d)
    q_seq = jnp.concatenate([q_seq] * (G * gp), axis=0)
    q_pos = jnp.concatenate([q_pos] * (G * gp), axis=0)
    qg = _stack_heads(q_ref[...], G * gp, d.hd)
    cl = lax.broadcasted_iota(I32, (1, TAIL), 1)

    def weighted(kblk, vblk, valid, run):
        z = jnp.concatenate(
            [_dot_nt(qg[c * RG:(c + 1) * RG], kblk[:, c * d.hd:(c + 1) * d.hd]) for c in range(gp)],
            axis=0) * d.scale
        a, tot = _sb_weights(z, valid, run)
        o = jnp.concatenate(
            [_dot(a[c * RG:(c + 1) * RG], vblk[:, c * d.hd:(c + 1) * d.hd]) for c in range(gp)],
            axis=0)
        return o, tot

    def block(kb, valid, run):
        start = pl.multiple_of(kb * TAIL, TAIL)
        return weighted(kp_ref[pl.ds(start, TAIL), :], vp_ref[pl.ds(start, TAIL), :], valid, run)

    o, tot = block(i, (d.n_meta + i * TAIL + cl) < q_pos, 0.0)
    acc_ref[...] = o
    run_ref[...] = tot

    def cond(c):
        kb, top = c
        return (kb >= 0) & (top > -SB_UNDERFLOW)

    def body(c):
        kb, _ = c
        o, tot = block(kb, None, run_ref[...])
        acc_ref[...] += o
        run = run_ref[...] + tot
        run_ref[...] = run
        return kb - 1, jnp.max(run)

    _, top = lax.while_loop(cond, body, (i - 1, jnp.max(tot)))

    @pl.when(top > -SB_UNDERFLOW)
    def _():
        tk_seq, tk_pos = _tail_key_meta(d)
        valid = (tk_seq == q_seq) & (tk_pos < q_pos)
        o, _ = weighted(kt_ref[...], vt_ref[...], valid, run_ref[...])
        acc_ref[...] += o

    for j in range(G * gp):
        o_ref[:, j * d.hd:(j + 1) * d.hd] = acc_ref[j * TAIL:(j + 1) * TAIL, :].astype(o_ref.dtype)


def sb_prompt_attention(zb, k_off, v_off, d):
    gp = SB_GROUPS_PER_STEP if d.n_kv % SB_GROUPS_PER_STEP == 0 else 1
    nblk = d.B * d.nqb + 1
    tailb = d.NP // TAIL
    gw = gp * d.G * d.hd
    kw = gp * d.hd
    assert k_off % kw == 0 and v_off % kw == 0
    kblk, vblk = k_off // kw, v_off // kw
    seq_of = lambda s: jnp.minimum(s // d.nqb, d.B - 1)
    in_specs = [
        pl.BlockSpec((TAIL, gw), lambda s, g: (s, g)),
        pl.BlockSpec((d.SEQ, kw), lambda s, g: (seq_of(s), kblk + g)),
        pl.BlockSpec((TAIL, kw), lambda s, g: (tailb, kblk + g)),
        pl.BlockSpec((d.SEQ, kw), lambda s, g: (seq_of(s), vblk + g)),
        pl.BlockSpec((TAIL, kw), lambda s, g: (tailb, vblk + g)),
    ]
    rows = gp * d.G * TAIL
    return pl.pallas_call(
        functools.partial(_sb_prompt_kernel, d=d, gp=gp),
        grid=(nblk, d.n_kv // gp),
        in_specs=in_specs,
        out_specs=pl.BlockSpec((TAIL, gw), lambda s, g: (s, g)),
        out_shape=jax.ShapeDtypeStruct((d.M, d.QW), BF16),
        scratch_shapes=[pltpu.VMEM((rows, d.hd), F32), pltpu.VMEM((rows, 1), F32)],
        compiler_params=_cparams(2),
    )(zb, zb, zb, zb, zb)


def _block_diag_q(q, d):
    R = q.shape[0]
    rho = lax.broadcasted_iota(I32, (R, 1), 0)
    grp = _div(_mod(rho, d.n_heads), d.G)
    zero = jnp.zeros_like(q)
    return jnp.concatenate([jnp.where(grp == g, q, zero) for g in range(d.n_kv)], axis=1)


def _own_group_slab(oall, d):
    R = oall.shape[0]
    rho = lax.broadcasted_iota(I32, (R, 1), 0)
    grp = _div(_mod(rho, d.n_heads), d.G)
    out = jnp.zeros((R, d.hd), F32)
    for g in range(d.n_kv):
        out = out + jnp.where(grp == g, oall[:, g * d.hd:(g + 1) * d.hd], 0.0)
    return out


def _page_rows(page_ref, d):
    return jnp.concatenate([page_ref[:, h, :] for h in range(d.n_kv)], axis=1).astype(BF16)


IDX_PAGES_PER_STEP = 8
ATTN_PAGES_PER_STEP = 8


def _pages_per_step(n_pages, target):
    return max(g for g in range(1, target + 1) if n_pages % g == 0)


def _page_index_map(layer, j, pg, npg, trailing):
    def index_map(b, p, pt):
        return (layer, pt[b, jnp.minimum(p * pg + j, npg - 1)]) + (0,) * trailing
    return index_map


def _sample_idx_kernel(pt_ref, qi_ref, wi_ref, *refs, d, pg):
    kc_refs, kt_ref, o_ref = refs[:pg], refs[pg], refs[pg + 1]
    p = pl.program_id(1)
    last = d.n_pages // pg

    def scores(kblk):
        w = jnp.maximum(_dot_nt(qi_ref[...], kblk), 0.0) * wi_ref[...]
        rows = [jnp.sum(w[t * d.nih:(t + 1) * d.nih], axis=0, keepdims=True) for t in range(d.Ts)]
        rows.append(jnp.zeros((SUBLANE - d.Ts, kblk.shape[0]), F32))
        return jnp.concatenate(rows, axis=0)

    @pl.when(p < last)
    def _():
        o_ref[...] = scores(jnp.concatenate([r[...].astype(BF16) for r in kc_refs], axis=0))

    @pl.when(p == last)
    def _():
        o_ref[...] = jnp.zeros_like(o_ref)
        o_ref[:, :TAIL] = scores(kt_ref[...])


def sample_idx_scores(page_table, qi_s, wi_s, cache_ki, layer, kib, d):
    R = d.Ts * d.nih
    npg = d.n_pages
    pg = _pages_per_step(npg, IDX_PAGES_PER_STEP)
    tailb = d.NP // TAIL
    grid_spec = pltpu.PrefetchScalarGridSpec(
        num_scalar_prefetch=1,
        grid=(d.Bd, npg // pg + 1),
        in_specs=[
            pl.BlockSpec((None, R, d.idim), lambda b, p, pt: (b, 0, 0)),
            pl.BlockSpec((None, R, 1), lambda b, p, pt: (b, 0, 0)),
        ] + [
            pl.BlockSpec((None, None, d.page, d.idim), _page_index_map(layer, j, pg, npg, 2))
            for j in range(pg)
        ] + [
            pl.BlockSpec((TAIL, d.idim), lambda b, p, pt: (tailb, 0)),
        ],
        out_specs=pl.BlockSpec((None, SUBLANE, pg * LANE), lambda b, p, pt: (b, 0, p)),
    )
    return pl.pallas_call(
        functools.partial(_sample_idx_kernel, d=d, pg=pg),
        grid_spec=grid_spec,
        out_shape=jax.ShapeDtypeStruct((d.Bd, SUBLANE, (npg + pg) * LANE), F32),
        compiler_params=_cparams(2),
    )(page_table, qi_s, wi_s, *([cache_ki] * pg), kib)


def _sample_select_kernel(s_ref, o_ref, *, d, n_sel):
    b = pl.program_id(0)
    L = s_ref.shape[1]
    col = lax.broadcasted_iota(I32, (1, L), 1)
    t = lax.broadcasted_iota(I32, (SUBLANE, 1), 0)
    snew = col - d.past - d.soff - d.Ts * b
    valid = (col < d.past) | ((snew >= 0) & (snew < d.Ts) & (snew <= t))
    valid = valid & (t < d.Ts)
    key = jnp.where(valid, _float_key(s_ref[...]), jnp.int32(INT_MIN))
    _topk_mask(key, n_sel, [(c0, min(L, c0 + TIE_CHUNK)) for c0 in range(0, L, TIE_CHUNK)], o_ref)


def sample_select(scores, d):
    n_sel = min(TOPK_MAX, (d.past + d.Ts) // 4)
    L = scores.shape[2]
    spec = pl.BlockSpec((None, SUBLANE, L), lambda b: (b, 0, 0))
    return pl.pallas_call(
        functools.partial(_sample_select_kernel, d=d, n_sel=n_sel),
        grid=(d.Bd,),
        in_specs=[spec],
        out_specs=spec,
        out_shape=jax.ShapeDtypeStruct(scores.shape, F32),
        compiler_params=_cparams(1),
    )(scores)


def _sample_dsa_attn_kernel(pt_ref, q_ref, sel_ref, *refs, d, pg):
    kc_refs, vc_refs = refs[:pg], refs[pg:2 * pg]
    kt_ref, vt_ref, o_ref, expand_ref, own_ref, m_ref, l_ref, acc_ref = refs[2 * pg:]
    p = pl.program_id(1)
    last = d.n_pages // pg
    R = q_ref.shape[0]
    W = d.page * d.n_kv

    @pl.when(p == 0)
    def _():
        key = lax.broadcasted_iota(I32, (d.page, W), 0)
        col = lax.broadcasted_iota(I32, (d.page, W), 1)
        expand_ref[...] = jnp.where(_div(col, d.n_kv) == key, 1.0, 0.0).astype(BF16)
        rho = lax.broadcasted_iota(I32, (R, W), 0)
        col = lax.broadcasted_iota(I32, (R, W), 1)
        own_ref[...] = jnp.where(_mod(col, d.n_kv) == _div(_mod(rho, d.n_heads), d.G), 1.0, 0.0)
        m_ref[...] = jnp.full_like(m_ref, NEG)
        l_ref[...] = jnp.zeros_like(l_ref)
        acc_ref[...] = jnp.zeros_like(acc_ref)

    def per_row(sel):
        return jnp.concatenate(
            [jnp.broadcast_to(sel[t:t + 1], (d.n_heads, sel.shape[1])) for t in range(d.Ts)], axis=0)

    def update(sc, mask, pv):
        sc = jnp.where(mask, sc, NEG)
        m_old = m_ref[...]
        m_new = jnp.maximum(m_old, jnp.max(sc, axis=1, keepdims=True))
        alpha = jnp.exp(m_old - m_new)
        pr = jnp.where(mask, jnp.exp(sc - m_new), 0.0)
        l_ref[...] = alpha * l_ref[...] + jnp.sum(pr, axis=1, keepdims=True)
        acc_ref[...] = alpha * acc_ref[...] + pv(pr.astype(BF16))
        m_ref[...] = m_new

    @pl.when(p < last)
    def _():
        kflat = jnp.concatenate([r[...].reshape(W, d.hd) for r in kc_refs], axis=0).astype(BF16)
        vflat = jnp.concatenate([r[...].reshape(W, d.hd) for r in vc_refs], axis=0).astype(BF16)
        sc = _dot_nt(q_ref[...], kflat) * d.scale
        sel = sel_ref[...].astype(BF16)
        selx = jnp.concatenate([_dot(sel[:, j * d.page:(j + 1) * d.page], expand_ref[...])
                                for j in range(pg)], axis=1)
        own = jnp.concatenate([own_ref[...]] * pg, axis=1)
        update(sc, per_row(selx) * own > 0.5, lambda pr: _dot(pr, vflat))

    @pl.when(p == last)
    def _():
        sc = _dot_nt(_block_diag_q(q_ref[...], d), kt_ref[...]) * d.scale
        update(sc, per_row(sel_ref[:, :TAIL]) > 0.5,
               lambda pr: _own_group_slab(_dot(pr, vt_ref[...]), d))
        o_ref[...] = (acc_ref[...] / l_ref[...]).astype(o_ref.dtype)


def sample_dsa_attention(page_table, q_s, sel, cache_k, cache_v, layer, kb, vb, d):
    R = d.Ts * d.n_heads
    npg = d.n_pages
    pg = _pages_per_step(npg, ATTN_PAGES_PER_STEP)
    assert sel.shape[2] >= (npg + pg) * LANE
    tailb = d.NP // TAIL
    page_specs = [pl.BlockSpec((None, None, d.page, d.n_kv, d.hd),
                               _page_index_map(layer, j, pg, npg, 3)) for j in range(pg)]
    tail_spec = pl.BlockSpec((TAIL, d.KVW), lambda b, p, pt: (tailb, 0))
    grid_spec = pltpu.PrefetchScalarGridSpec(
        num_scalar_prefetch=1,
        grid=(d.Bd, npg // pg + 1),
        in_specs=[
            pl.BlockSpec((None, R, d.hd), lambda b, p, pt: (b, 0, 0)),
            pl.BlockSpec((None, SUBLANE, pg * LANE), lambda b, p, pt: (b, 0, p)),
        ] + page_specs + page_specs + [tail_spec, tail_spec],
        out_specs=pl.BlockSpec((None, R, d.hd), lambda b, p, pt: (b, 0, 0)),
        scratch_shapes=[pltpu.VMEM((d.page, d.page * d.n_kv), BF16),
                        pltpu.VMEM((R, d.page * d.n_kv), F32), pltpu.VMEM((R, 1), F32),
                        pltpu.VMEM((R, 1), F32), pltpu.VMEM((R, d.hd), F32)],
    )
    return pl.pallas_call(
        functools.partial(_sample_dsa_attn_kernel, d=d, pg=pg),
        grid_spec=grid_spec,
        out_shape=jax.ShapeDtypeStruct((d.Bd, R, d.hd), BF16),
        compiler_params=_cparams(2),
    )(page_table, q_s, sel, *([cache_k] * pg), *([cache_v] * pg), kb, vb)


def _sample_sb_kernel(pt_ref, q_ref, kt_ref, vt_ref, kc_hbm, vc_hbm, o_ref,
                      kbuf, vbuf, sems, *, d, layer):
    b = pl.program_id(0)
    R = q_ref.shape[0]
    qbd = _block_diag_q(q_ref[...], d)

    def block(kblk, vblk, valid, run):
        a, tot = _sb_weights(_dot_nt(qbd, kblk) * d.scale, valid, run)
        return _own_group_slab(_dot(a, vblk), d), tot

    rho = lax.broadcasted_iota(I32, (R, 1), 0)
    tq = _div(rho, d.n_heads)
    c = lax.broadcasted_iota(I32, (1, TAIL), 1)
    snew = c - d.soff - d.Ts * b
    acc0, run0 = block(kt_ref[...], vt_ref[...], (snew >= 0) & (snew < tq), 0.0)

    def page_copies(page):
        return (pltpu.make_async_copy(kc_hbm.at[layer, page], kbuf, sems.at[0]),
                pltpu.make_async_copy(vc_hbm.at[layer, page], vbuf, sems.at[1]))

    def cond(carry):
        p, top, _, _ = carry
        return (p >= 0) & (top > -SB_UNDERFLOW)

    def body(carry):
        p, _, acc, run = carry
        copies = page_copies(pt_ref[b, p])
        for cp in copies:
            cp.start()
        for cp in copies:
            cp.wait()
        od, tot = block(_page_rows(kbuf, d), _page_rows(vbuf, d), None, run)
        run = run + tot
        return p - 1, jnp.max(run), acc + od, run

    _, _, acc, _ = lax.while_loop(cond, body,
                                  (jnp.int32(d.n_pages - 1), jnp.max(run0), acc0, run0))
    o_ref[...] = acc.astype(o_ref.dtype)


def sample_sb_attention(page_table, q_s, cache_k, cache_v, layer, zb, k_off, v_off, d):
    R = d.Ts * d.n_heads
    tailb = d.NP // TAIL
    page_shape = (d.page, d.n_kv, d.hd)
    grid_spec = pltpu.PrefetchScalarGridSpec(
        num_scalar_prefetch=1,
        grid=(d.Bd,),
        in_specs=[
            pl.BlockSpec((None, R, d.hd), lambda b, pt: (b, 0, 0)),
            pl.BlockSpec((TAIL, d.KVW), lambda b, pt: (tailb, k_off)),
            pl.BlockSpec((TAIL, d.KVW), lambda b, pt: (tailb, v_off)),
            pl.BlockSpec(memory_space=pl.ANY),
            pl.BlockSpec(memory_space=pl.ANY),
        ],
        out_specs=pl.BlockSpec((None, R, d.hd), lambda b, pt: (b, 0, 0)),
        scratch_shapes=[pltpu.VMEM(page_shape, F32), pltpu.VMEM(page_shape, F32),
                        pltpu.SemaphoreType.DMA((2,))],
    )
    return pl.pallas_call(
        functools.partial(_sample_sb_kernel, d=d, layer=layer),
        grid_spec=grid_spec,
        out_shape=jax.ShapeDtypeStruct((d.Bd, R, d.hd), BF16),
        compiler_params=_cparams(1),
    )(page_table, q_s, zb, zb, cache_k, cache_v)


def _conv_glu_kernel(g_ref, halo_ref, u_ref, w_ref, b_ref, p1_ref, p2_ref, o_ref, *, d):
    m = pl.program_id(1)
    is_tail = m == d.NP // TAIL
    g = g_ref[...]
    gext = jnp.concatenate([halo_ref[...], g], axis=0)
    g1 = pltpu.roll(gext, 1, 0)[SUBLANE:]
    g2 = pltpu.roll(gext, 2, 0)[SUBLANE:]
    r = lax.broadcasted_iota(I32, (TAIL, 1), 0)
    step = jnp.where(r < d.soff, _mod(r, d.n_meta), _mod(r - d.soff, d.Ts))
    g1 = jnp.where(is_tail & (step < 1), p1_ref[...], g1)
    g2 = jnp.where(is_tail & (step < 2), p2_ref[...], g2)
    gc = b_ref[...] + g2 * w_ref[0:1, :] + g1 * w_ref[1:2, :] + g * w_ref[2:3, :]
    o_ref[...] = (gc * jax.nn.sigmoid(gc) * u_ref[...]).astype(o_ref.dtype)


def conv_glu_gate(gu, conv_w, conv_b, layer, prev1, prev2, d):
    assert conv_w.shape[1] == 3
    M = d.M
    tf = _pick_tile(d.d_ff, 5504, LANE)
    nf = d.d_ff // tf
    per_seq = d.nqb
    tail_m = d.NP // TAIL

    def halo_idx(f, m):
        b = m // per_seq
        seq_start = (d.NP + b * d.n_meta + d.n_meta - SUBLANE) // SUBLANE
        normal = (TAIL // SUBLANE) * m - 1
        idx = jnp.where(m % per_seq == 0, seq_start, normal)
        return (jnp.where(m == tail_m, 0, idx), f)

    return pl.pallas_call(
        functools.partial(_conv_glu_kernel, d=d),
        grid=(nf, M // TAIL),
        in_specs=[
            pl.BlockSpec((TAIL, tf), lambda f, m: (m, f)),
            pl.BlockSpec((SUBLANE, tf), halo_idx),
            pl.BlockSpec((TAIL, tf), lambda f, m: (m, nf + f)),
            pl.BlockSpec((None, 3, tf), lambda f, m: (layer, 0, f)),
            pl.BlockSpec((None, 1, tf), lambda f, m: (layer, 0, f)),
            pl.BlockSpec((TAIL, tf), lambda f, m: (0, f)),
            pl.BlockSpec((TAIL, tf), lambda f, m: (0, f)),
        ],
        out_specs=pl.BlockSpec((TAIL, tf), lambda f, m: (m, f)),
        out_shape=jax.ShapeDtypeStruct((M, d.d_ff), BF16),
        compiler_params=_cparams(2),
    )(gu, gu, gu, conv_w, conv_b[:, None, :], prev1, prev2)


def _rope_tables(pos, hd):
    rot = hd // 4
    half = rot // 2
    inv = ROPE_THETA ** (-jnp.arange(half, dtype=F32) / half)
    ang = pos.astype(F32)[:, None] * inv[None, :]
    cos, sin = jnp.cos(ang), jnp.sin(ang)
    n = pos.shape[0]
    ones = jnp.ones((n, hd - rot), F32)
    zeros = jnp.zeros((n, hd - rot), F32)
    zh = jnp.zeros((n, half), F32)
    c = jnp.concatenate([cos, cos, ones], axis=1)
    s1 = jnp.concatenate([zh, sin, zeros], axis=1)
    s2 = jnp.concatenate([-sin, zh, zeros], axis=1)
    return c, s1, s2


def _to_seq(rows, d):
    W = rows.shape[1]
    meta = rows[d.NP: d.NP + d.soff].reshape(d.B, d.n_meta, W)
    prm = rows[: d.NP].reshape(d.B, d.SEQ, W)
    smp = rows[d.NP + d.soff: d.NP + d.soff + d.Bd * d.Ts].reshape(d.Bd, d.Ts, W)
    return jnp.concatenate([meta, prm], axis=1), smp


def _sample_rows(x, d, per_row):
    w = x.shape[1] // per_row
    lo = d.NP + d.soff
    return x[lo: lo + d.Bd * d.Ts].reshape(d.Bd, d.Ts * per_row, w)


def _with_sample_rows(o, o_s, d):
    return lax.dynamic_update_slice(o, o_s.reshape(d.Bd * d.Ts, d.QW), (d.NP + d.soff, 0))


def kernel(x_prompt, x_sample, cache_k_a, cache_v_a, cache_kidx_a, cache_k_b, cache_v_b,
           state_ffn_conv, page_table, meta_tokens, norm_mix, norm_ffn, w_in_a, q_norm_a,
           k_norm_a, w_out_a, w_in_b, w_out_b, w_gate_up, conv_w, conv_b, w_down):
    B, SEQ, D = x_prompt.shape
    Bd, Ts, _ = x_sample.shape
    n_meta = meta_tokens.shape[0]
    _, n_pool, page, n_kv, hd = cache_k_a.shape
    idim = cache_kidx_a.shape[-1]
    QW = w_out_a.shape[1]
    nih = (w_in_a.shape[2] - QW - 2 * n_kv * hd - idim) // idim
    d = Dims(B=B, SEQ=SEQ, D=D, Bd=Bd, Ts=Ts, n_meta=n_meta, page=page,
             n_pages=page_table.shape[1], n_kv=n_kv, hd=hd, n_heads=QW // hd, nih=nih,
             idim=idim, d_ff=w_down.shape[1], depth=norm_mix.shape[0])
    assert SEQ % TAIL == 0 and page == LANE and hd == LANE and idim == LANE
    assert d.soff + Bd * Ts <= TAIL and n_meta % SUBLANE == 0 and 2 <= Ts <= SUBLANE
    assert w_in_a.shape[2] == d.QW + 2 * d.KVW + d.IQW + idim + nih
    assert (Ts * d.nih) % SUBLANE == 0 and d.nih % SUBLANE == 0
    npad = TAIL - d.soff - Bd * Ts

    h = jnp.concatenate([x_prompt.reshape(d.NP, D), jnp.tile(meta_tokens, (B, 1)),
                         x_sample.reshape(Bd * Ts, D), jnp.zeros((npad, D), F32)], axis=0)
    pos = jnp.concatenate([jnp.tile(n_meta + jnp.arange(SEQ), B), jnp.tile(jnp.arange(n_meta), B),
                           jnp.tile(d.past + jnp.arange(Ts), Bd), jnp.zeros((npad,), I32)])
    tabs = _rope_tables(pos, hd)

    outs = {k: [] for k in ("ka_p", "va_p", "ki_p", "kb_p", "vb_p", "cv_p",
                            "ka_s", "va_s", "ki_s", "kb_s", "vb_s", "cv_s")}
    for layer in range(d.depth):
        j = layer // 2
        n = rmsnorm_bf16(h, norm_mix, layer)
        if layer % 2 == 0:
            z = matmul(n, w_in_a, j, transposed_w=True)
            q, kf, kb, vb, qi, kif, kib, wi, kt, kit = dsa_post(z, q_norm_a, k_norm_a, j, tabs, d)
            o = dsa_prompt_attention(q, kt, vb, qi, kit, wi, d)
            qi_rows = jnp.transpose(qi[:, d.NP + d.soff: d.NP + d.soff + Bd * Ts], (1, 0, 2))
            qi_s = qi_rows.reshape(Bd, Ts * d.nih, idim)
            wi_s = _sample_rows(wi, d, d.nih).reshape(Bd, Ts * d.nih, 1)
            scores = sample_idx_scores(page_table, qi_s, wi_s, cache_kidx_a, j, kib, d)
            sel = sample_select(scores, d)
            o_s = sample_dsa_attention(page_table, _sample_rows(q, d, d.n_heads), sel,
                                       cache_k_a, cache_v_a, j, kb, vb, d)
            o = _with_sample_rows(o, o_s, d)
            vf = z[:, d.QW + d.KVW: d.QW + 2 * d.KVW]
            for name, rows in (("ka", kf), ("va", vf), ("ki", kif)):
                p_, s_ = _to_seq(rows, d)
                outs[name + "_p"].append(p_)
                outs[name + "_s"].append(s_)
            w_out = w_out_a
        else:
            z, zb = matmul(n, w_in_b, j, emit_bf16=True)
            o = sb_prompt_attention(zb, d.QW, d.QW + d.KVW, d)
            lo = d.NP + d.soff
            q_s = zb[lo: lo + Bd * Ts, :d.QW].reshape(Bd, Ts * d.n_heads, hd)
            o_s = sample_sb_attention(page_table, q_s, cache_k_b, cache_v_b, j, zb,
                                      d.QW // d.KVW, d.QW // d.KVW + 1, d)
            o = _with_sample_rows(o, o_s, d)
            for name, lo in (("kb", d.QW), ("vb", d.QW + d.KVW)):
                p_, s_ = _to_seq(z[:, lo: lo + d.KVW], d)
                outs[name + "_p"].append(p_)
                outs[name + "_s"].append(s_)
            w_out = w_out_b
        h = matmul(o, w_out, j, resid=h)

        n = rmsnorm_bf16(h, norm_ffn, layer)
        gu = matmul(n, w_gate_up, layer)
        st = state_ffn_conv[layer]
        zrow = jnp.zeros((Bd, 1, d.d_ff), F32)
        pad_s = jnp.zeros((Bd, Ts - 2, d.d_ff), F32)
        p1 = jnp.concatenate([st[:, 1:2], zrow, pad_s], axis=1).reshape(Bd * Ts, d.d_ff)
        p2 = jnp.concatenate([st[:, 0:1], st[:, 1:2], pad_s], axis=1).reshape(Bd * Ts, d.d_ff)
        top = jnp.zeros((d.soff, d.d_ff), F32)
        bot = jnp.zeros((npad, d.d_ff), F32)
        prev1 = jnp.concatenate([top, p1, bot], axis=0)
        prev2 = jnp.concatenate([top, p2, bot], axis=0)
        act = conv_glu_gate(gu, conv_w, conv_b, layer, prev1, prev2, d)
        h = matmul(act, w_down, layer, k_block=0, k_blocks=2, resid=h)
        h = matmul(act, w_down, layer, k_block=1, k_blocks=2, resid=h)
        lo = d.NP + d.soff
        outs["cv_p"].append(jnp.stack(
            [gu[(b + 1) * SEQ - 2: (b + 1) * SEQ, :d.d_ff] for b in range(B)]))
        outs["cv_s"].append(gu[lo: lo + Bd * Ts, :d.d_ff].reshape(Bd, Ts, d.d_ff)[:, -2:])

    def heads(xs):
        a = jnp.stack(xs)
        return a.reshape(a.shape[:3] + (d.n_kv, d.hd))

    y_p = h[: d.NP].reshape(B, SEQ, D)
    y_s = h[d.NP + d.soff: d.NP + d.soff + Bd * Ts].reshape(Bd, Ts, D)
    return (y_p, y_s,
            heads(outs["ka_p"]), heads(outs["va_p"]), jnp.stack(outs["ki_p"]),
            heads(outs["kb_p"]), heads(outs["vb_p"]), jnp.stack(outs["cv_p"]),
            heads(outs["ka_s"]), heads(outs["va_s"]), jnp.stack(outs["ki_s"]),
            heads(outs["kb_s"]), heads(outs["vb_s"]), jnp.stack(outs["cv_s"]))
```
